```python
import functools
import jax, jax.numpy as jnp
from jax import lax
import numpy as np

D_MODEL = 1024
BATCH = 16
SEQ = 2048
DEPTH = 1
DEC_BATCH = 32
DEC_SEQ = 8
PAST_LEN = 16384
PAGE_SIZE = 128

D_MIX = D_MODEL
HD_A = 64
H_A = (D_MIX // 2) // HD_A
PATTERNS = ((128, 1), (512, 4), (2048, 16))
MAX_WINDOW = max(w for w, _ in PATTERNS)
WIN_BUF = min(MAX_WINDOW, PAST_LEN)
H_B = 4
DV_B = (D_MIX - H_A * HD_A) // H_B
DK_B = DV_B // 2
GATE_RANK = 16
GLA_TAU = 16.0
CHUNK = 16
D_FF = 11 * D_MODEL // 4
CONV_W = 3
EPS = 1e-6
NEG = -1e30
SPLITS = (H_A * HD_A, H_A * HD_A, H_A * HD_A, H_B * DK_B, H_B * DK_B, H_B * DV_B, H_B * DV_B, GATE_RANK)
D_IN = sum(SPLITS)

kernel_name = "hymba_dilated_gla_convffn_step"


def rmsnorm(x, g):
    xf = x.astype(jnp.float32)
    y = xf * lax.rsqrt(jnp.mean(xf * xf, axis=-1, keepdims=True) + EPS)
    return (y * g.astype(jnp.float32)).astype(x.dtype)


def dilated_prompt(q, k, v, window, dil):
    B, S, H, E = q.shape
    n = window // dil
    span = n * dil
    Sp = -(-S // span) * span
    nb = Sp // span

    def to_blocks(t):
        t = jnp.pad(t.astype(jnp.float32), ((0, 0), (0, Sp - S), (0, 0), (0, 0)))
        t = t.reshape(B, nb, n, dil, H, E)
        return t.transpose(0, 3, 4, 1, 2, 5)

    def with_prev(t):
        prev = jnp.pad(t, ((0, 0), (0, 0), (0, 0), (1, 0), (0, 0), (0, 0)))[:, :, :, :-1]
        return jnp.concatenate([prev, t], axis=4)

    qb, kb, vb = to_blocks(q), to_blocks(k), to_blocks(v)
    kk, vv = with_prev(kb), with_prev(vb)
    s = jnp.einsum('brhcie,brhcje->brhcij', qb, kk) * (E ** -0.5)
    i = jnp.arange(n)[:, None]
    j = jnp.arange(2 * n)[None, :]
    delta = n + i - j
    band = (delta >= 0) & (delta <= n)
    first = (jnp.arange(nb) == 0)[:, None, None] & (j < n)[None]
    mask = band[None] & ~first
    s = jnp.where(mask, s, NEG)
    lse = jax.nn.logsumexp(s, axis=-1)
    p = jnp.exp(s - lse[..., None])
    o = jnp.einsum('brhcij,brhcje->brhcie', p, vv)
    o = o.transpose(0, 3, 4, 1, 2, 5).reshape(B, Sp, H, E)[:, :S]
    lse = lse.transpose(0, 3, 4, 1, 2).reshape(B, Sp, H)[:, :S]
    return o, lse


def dilated_sample(q, kc, vc, window, dil, lb):
    B, T, H, E = q.shape
    n = window // dil
    idx = lb + jnp.arange(T)[:, None] - dil * jnp.arange(n + 1)[None, :]
    valid = idx >= 0
    idxc = jnp.maximum(idx, 0).reshape(-1)
    kg = jnp.take(kc, idxc, axis=1).reshape(B, T, n + 1, H, E).astype(jnp.float32)
    vg = jnp.take(vc, idxc, axis=1).reshape(B, T, n + 1, H, E).astype(jnp.float32)
    s = jnp.einsum('bthe,btjhe->bhtj', q.astype(jnp.float32), kg) * (E ** -0.5)
    s = jnp.where(valid[None, None], s, NEG)
    lse = jax.nn.logsumexp(s, axis=-1)
    p = jnp.exp(s - lse[..., None])
    o = jnp.einsum('bhtj,btjhe->bthe', p, vg)
    return o, lse.transpose(0, 2, 1)


def combine_patterns(results):
    outs = jnp.stack([o for o, _ in results], axis=0)
    lses = jnp.stack([l for _, l in results], axis=0)
    w = jax.nn.softmax(lses, axis=0)
    return jnp.sum(w[..., None] * outs, axis=0)


def prompt_attn(q, k, v):
    return combine_patterns([dilated_prompt(q, k, v, w, d) for w, d in PATTERNS])


def sample_attn(q, k, v, ck, cv):
    kc = jnp.concatenate([ck.astype(k.dtype), k], axis=1)
    vc = jnp.concatenate([cv.astype(v.dtype), v], axis=1)
    lb = ck.shape[1]
    return combine_patterns([dilated_sample(q, kc, vc, w, d, lb) for w, d in PATTERNS])


def gla_recurrent(q, k, v, logf, s0):
    B, T, H, DK = q.shape
    Tp = -(-T // CHUNK) * CHUNK
    nC = Tp // CHUNK

    def chunks(t):
        t = jnp.pad(t.astype(jnp.float32), ((0, 0), (0, Tp - T), (0, 0), (0, 0)))
        return t.reshape(B, nC, CHUNK, H, t.shape[-1]).transpose(1, 0, 3, 2, 4)

    qc = chunks(q * (DK ** -0.5))
    kc, vc, gc = chunks(k), chunks(v), chunks(logf)
    causal = jnp.tril(jnp.ones((CHUNK, CHUNK), dtype=bool))

    def step(S, inp):
        qi, ki, vi, gi = inp
        b = jnp.cumsum(gi, axis=2)
        qe = qi * jnp.exp(b)
        ke = ki * jnp.exp(-b)
        att = jnp.where(causal, jnp.einsum('bhik,bhjk->bhij', qe, ke), 0.0)
        o = jnp.einsum('bhij,bhjv->bhiv', att, vi) + jnp.einsum('bhik,bhkv->bhiv', qe, S)
        bl = b[:, :, -1:, :]
        S = jnp.exp(bl[:, :, 0, :])[..., None] * S + jnp.einsum('bhjk,bhjv->bhkv', ki * jnp.exp(bl - b), vi)
        return S, o

    S, o = lax.scan(step, s0.astype(jnp.float32), (qc, kc, vc, gc))
    o = o.transpose(1, 0, 3, 2, 4).reshape(B, Tp, H, -1)[:, :T]
    return o, S


def conv_ffn(xn, conv_buf, w_up, conv_w, conv_b, w_down):
    T = xn.shape[1]
    up = xn @ w_up
    g, u = jnp.split(up, 2, axis=-1)
    gp = jnp.concatenate([conv_buf.astype(g.dtype), g], axis=1)
    c = conv_b + sum(conv_w[j] * gp[:, j:j + T] for j in range(CONV_W))
    h = jax.nn.gelu(c, approximate=False) * u
    return h @ w_down, gp[:, T:]


def layer(x, attn_fn, s_gla, conv_buf, w_in, w_a2, b_a, g_gla_norm, w_o,
          g_pre_mix, g_post_mix, g_pre_ffn, g_post_ffn, w_up, conv_w, conv_b, w_down):
    B, T, _ = x.shape
    xn = rmsnorm(x, g_pre_mix)
    proj = xn @ w_in
    split_idx = [int(i) for i in np.cumsum(SPLITS)[:-1]]
    q_a, k_a, v_a, q_b, k_b, v_b, r_b, a_lr = jnp.split(proj, split_idx, axis=-1)
    heads = lambda t, h: t.reshape(B, T, h, -1)
    q_a, k_a, v_a = heads(q_a, H_A), heads(k_a, H_A), heads(v_a, H_A)
    o_a = attn_fn(q_a, k_a, v_a)
    logf = jax.nn.log_sigmoid((a_lr @ w_a2 + b_a).astype(jnp.float32)) / GLA_TAU
    o_b, s_new = gla_recurrent(heads(q_b, H_B), heads(k_b, H_B), heads(v_b, H_B), heads(logf, H_B), s_gla)
    o_b = rmsnorm(o_b, g_gla_norm) * jax.nn.silu(heads(r_b, H_B).astype(jnp.float32))
    mixed = jnp.concatenate([o_a.reshape(B, T, -1), o_b.reshape(B, T, -1)], axis=-1).astype(x.dtype) @ w_o
    x = x + rmsnorm(mixed, g_post_mix)
    f, conv_new = conv_ffn(rmsnorm(x, g_pre_ffn), conv_buf, w_up, conv_w, conv_b, w_down)
    x = x + rmsnorm(f, g_post_ffn)
    return x, k_a, v_a, s_new, conv_new


def setup_inputs(seed: int = 0) -> dict:
    key = jax.random.key(seed)
    ks = jax.random.split(key, 20)
    nrm = lambda k, shape, scale: jax.random.normal(k, shape, jnp.float32) * scale
    return {
        "x_prompt": nrm(ks[0], (BATCH, SEQ, D_MODEL), 1.0),
        "x_sample": nrm(ks[1], (DEC_BATCH, DEC_SEQ, D_MODEL), 1.0),
        "cache_k_win": nrm(ks[2], (DEPTH, DEC_BATCH, WIN_BUF, H_A, HD_A), 1.0),
        "cache_v_win": nrm(ks[3], (DEPTH, DEC_BATCH, WIN_BUF, H_A, HD_A), 1.0),
        "state_gla": nrm(ks[4], (DEPTH, DEC_BATCH, H_B, DK_B, DV_B), 1.0),
        "state_ffn_conv": nrm(ks[5], (DEPTH, DEC_BATCH, CONV_W - 1, D_FF), 1.0),
        "w_in": nrm(ks[6], (DEPTH, D_MODEL, D_IN), D_MODEL ** -0.5),
        "w_a2": nrm(ks[7], (DEPTH, GATE_RANK, H_B * DK_B), GATE_RANK ** -0.5),
        "b_a": 2.0 + nrm(ks[8], (DEPTH, H_B * DK_B), 0.5),
        "g_gla_norm": 1.0 + nrm(ks[9], (DEPTH, H_B, DV_B), 0.05),
        "w_o": nrm(ks[10], (DEPTH, D_MIX, D_MODEL), D_MIX ** -0.5),
        "g_pre_mix": 1.0 + nrm(ks[11], (DEPTH, D_MODEL), 0.05),
        "g_post_mix": 1.0 + nrm(ks[12], (DEPTH, D_MODEL), 0.05),
        "g_pre_ffn": 1.0 + nrm(ks[13], (DEPTH, D_MODEL), 0.05),
        "g_post_ffn": 1.0 + nrm(ks[14], (DEPTH, D_MODEL), 0.05),
        "w_up": nrm(ks[15], (DEPTH, D_MODEL, 2 * D_FF), D_MODEL ** -0.5),
        "conv_w": nrm(ks[16], (DEPTH, CONV_W, D_FF), CONV_W ** -0.5),
        "conv_b": nrm(ks[17], (DEPTH, D_FF), 0.02),
        "w_down": nrm(ks[18], (DEPTH, D_FF, D_MODEL), D_FF ** -0.5),
    }


def reference(x_prompt, x_sample, cache_k_win, cache_v_win, state_gla, state_ffn_conv,
              w_in, w_a2, b_a, g_gla_norm, w_o, g_pre_mix, g_post_mix, g_pre_ffn, g_post_ffn,
              w_up, conv_w, conv_b, w_down):
    xp, xs = x_prompt, x_sample
    Bp, Sp_len, _ = x_prompt.shape
    prompt_buf = min(MAX_WINDOW, Sp_len)
    kp_l, vp_l, sp_l, cp_l = [], [], [], []
    ks_l, vs_l, ss_l, cs_l = [], [], [], []
    for l in range(DEPTH):
        params = (w_in[l], w_a2[l], b_a[l], g_gla_norm[l], w_o[l], g_pre_mix[l], g_post_mix[l],
                  g_pre_ffn[l], g_post_ffn[l], w_up[l], conv_w[l], conv_b[l], w_down[l])
        s0 = jnp.zeros((Bp, H_B, DK_B, DV_B), jnp.float32)
        c0 = jnp.zeros((Bp, CONV_W - 1, D_FF), xp.dtype)
        xp, k_a, v_a, s_new, c_new = layer(xp, prompt_attn, s0, c0, *params)
        kp_l.append(k_a[:, Sp_len - prompt_buf:])
        vp_l.append(v_a[:, Sp_len - prompt_buf:])
        sp_l.append(s_new)
        cp_l.append(c_new)
        attn_s = functools.partial(sample_attn, ck=cache_k_win[l], cv=cache_v_win[l])
        xs, k_a, v_a, s_new, c_new = layer(xs, attn_s, state_gla[l], state_ffn_conv[l], *params)
        ks_l.append(k_a)
        vs_l.append(v_a)
        ss_l.append(s_new)
        cs_l.append(c_new)
    return (xp, xs,
            jnp.stack(kp_l), jnp.stack(vp_l), jnp.stack(sp_l), jnp.stack(cp_l),
            jnp.stack(ks_l), jnp.stack(vs_l), jnp.stack(ss_l), jnp.stack(cs_l))
```

```python
import functools

import jax
import jax.numpy as jnp
from jax import lax
from jax.experimental import pallas as pl
from jax.experimental.pallas import tpu as pltpu

F32 = jnp.float32
BF16 = jnp.bfloat16

EPS = 1e-6
NEG = -1e30
HD_A = 64
H_A = 8
H_B = 4
DK_B = 64
DV_B = 128
GATE_RANK = 16
GLA_TAU = 16.0
PATTERNS = ((128, 1), (512, 4), (2048, 16))
BAND = 128
GLA_CHUNK = 32
LANES = 128
VMEM_LIMIT = 56 * 1024 * 1024

D_A = H_A * HD_A
D_QKB = H_B * DK_B
D_VB = H_B * DV_B
D_MAIN = 3 * D_A + 2 * D_QKB + 2 * D_VB


def _dot(a, b):
    return jnp.dot(a, b, preferred_element_type=F32)


def _dot_nt(a, b):
    return lax.dot_general(a, b, (((1,), (1,)), ((), ())), preferred_element_type=F32)


def _dot_tn(a, b):
    return lax.dot_general(a, b, (((0,), (0,)), ((), ())), preferred_element_type=F32)


def _rms(x, g):
    return x * lax.rsqrt(jnp.mean(x * x, axis=-1, keepdims=True) + EPS) * g


def _params(*sem):
    return pltpu.CompilerParams(dimension_semantics=sem, vmem_limit_bytes=VMEM_LIMIT)


def _const_spec(shape):
    nd = len(shape)
    return pl.BlockSpec(shape, lambda *_: (0,) * nd, pipeline_mode=pl.Buffered(1))


def _inproj_kernel(x_ref, g_ref, w_ref, wlr_ref, wa2_ref, ba_ref,
                   qa_ref, ka_ref, va_ref, qb_ref, kb_ref, vb_ref, rb_ref, lf_ref):
    xn = _rms(x_ref[...], g_ref[...]).astype(BF16)
    off = 0
    for ref in (qa_ref, ka_ref, va_ref, qb_ref, kb_ref, vb_ref, rb_ref):
        n = ref.shape[-1]
        ref[...] = _dot(xn, w_ref[:, off:off + n])
        off += n
    a_lr = _dot(xn, wlr_ref[...])
    z = _dot(a_lr.astype(BF16), wa2_ref[...]) + ba_ref[...]
    lf_ref[...] = (jnp.minimum(z, 0.0) - jnp.log1p(jnp.exp(-jnp.abs(z)))) * (1.0 / GLA_TAU)


def _inproj(x2, g_pre, w_main, w_lr, w_a2, b_a, tm):
    n, d = x2.shape
    widths = (D_A, D_A, D_A, D_QKB, D_QKB, D_VB, D_VB, D_QKB)
    row = lambda w: pl.BlockSpec((tm, w), lambda i: (i, 0))
    return pl.pallas_call(
        _inproj_kernel,
        grid=(n // tm,),
        in_specs=[row(d), _const_spec(g_pre.shape), _const_spec(w_main.shape), _const_spec(w_lr.shape),
                  _const_spec(w_a2.shape), _const_spec(b_a.shape)],
        out_specs=[row(w) for w in widths],
        out_shape=[jax.ShapeDtypeStruct((n, w), F32) for w in widths],
        compiler_params=_params("parallel"),
        name="inproj",
    )(x2, g_pre, w_main, w_lr, w_a2, b_a)


def _attn_prompt_kernel(q_ref, k_ref, v_ref, o_ref, m_scr, l_scr, acc_scr):
    seq = q_ref.shape[0]
    n = BAND
    row = lax.broadcasted_iota(jnp.int32, (n, n), 0)
    col = lax.broadcasted_iota(jnp.int32, (n, n), 1)
    own_ok = col <= row
    prev_ok = col >= row
    is_a = lax.broadcasted_iota(jnp.int32, (n, LANES), 1) < HD_A

    for p, (_, dil) in enumerate(PATTERNS):
        nb = seq // (n * dil)

        def rows(start):
            if dil == 1:
                return pl.ds(pl.multiple_of(start, n), n)
            return pl.ds(start, n, stride=dil)

        def unit(u, carry, dil=dil, nb=nb, p=p, rows=rows):
            r = u // nb
            c = u % nb
            start = r + c * (n * dil)
            sl = rows(start)
            q = q_ref[sl, :] * (HD_A ** -0.5)
            k_own = k_ref[sl, :].astype(BF16)
            v_own = v_ref[sl, :].astype(BF16)
            if nb > 1:
                has_prev = c > 0
                slp = rows(jnp.where(has_prev, start - n * dil, start))
                k_prev = k_ref[slp, :].astype(BF16)
                v_prev = v_ref[slp, :].astype(BF16)
                prev_mask = jnp.logical_and(prev_ok, has_prev)

            def head(qh):
                s_own = jnp.where(own_ok, _dot_nt(qh, k_own), NEG)
                m = jnp.max(s_own, axis=-1, keepdims=True)
                if nb > 1:
                    s_prev = jnp.where(prev_mask, _dot_nt(qh, k_prev), NEG)
                    m = jnp.maximum(m, jnp.max(s_prev, axis=-1, keepdims=True))
                p_own = jnp.exp(s_own - m)
                l = jnp.sum(p_own, axis=-1, keepdims=True)
                o = _dot(p_own.astype(BF16), v_own)
                if nb > 1:
                    p_prev = jnp.exp(s_prev - m)
                    l = l + jnp.sum(p_prev, axis=-1, keepdims=True)
                    o = o + _dot(p_prev.astype(BF16), v_prev)
                return m, l, o

            ma, la, oa = head(jnp.where(is_a, q, 0.0).astype(BF16))
            mb, lb, ob = head(jnp.where(is_a, 0.0, q).astype(BF16))
            m2 = jnp.where(is_a, ma, mb)
            l2 = jnp.where(is_a, la, lb)
            o2 = jnp.where(is_a, oa, ob)
            if p == 0:
                m_scr[sl, :] = m2
                l_scr[sl, :] = l2
                acc_scr[sl, :] = o2
            else:
                m0 = m_scr[sl, :]
                mn = jnp.maximum(m0, m2)
                a0 = jnp.exp(m0 - mn)
                a2 = jnp.exp(m2 - mn)
                m_scr[sl, :] = mn
                l_scr[sl, :] = a0 * l_scr[sl, :] + a2 * l2
                acc_scr[sl, :] = a0 * acc_scr[sl, :] + a2 * o2
            return carry

        lax.fori_loop(0, dil * nb, unit, 0)

    o_ref[...] = (acc_scr[...] / l_scr[...]).astype(o_ref.dtype)


def _attn_prompt(qa, ka, va):
    b, s, _ = qa.shape
    assert s % (BAND * max(d for _, d in PATTERNS)) == 0
    blk = pl.BlockSpec((None, s, LANES), lambda i, j: (i, 0, j))
    return pl.pallas_call(
        _attn_prompt_kernel,
        grid=(b, D_A // LANES),
        in_specs=[blk, blk, blk],
        out_specs=blk,
        out_shape=jax.ShapeDtypeStruct((b, s, D_A), BF16),
        scratch_shapes=[pltpu.VMEM((s, LANES), F32)] * 3,
        compiler_params=_params("parallel", "parallel"),
        name="attn_prompt",
    )(qa, ka, va)


def _pattern_count(dist):
    cnt = jnp.zeros(dist.shape, jnp.int32)
    for _, dil in PATTERNS:
        shift = dil.bit_length() - 1
        hit = jnp.logical_and((dist & (dil - 1)) == 0, (dist >> shift) <= BAND)
        cnt = cnt + hit.astype(jnp.int32)
    return jnp.where(dist >= 0, cnt, 0)


def _attn_sample_kernel(q_ref, kn_ref, vn_ref, ck_ref, cv_ref, o_ref):
    t = q_ref.shape[0]
    lb = ck_ref.shape[0]
    is_a = lax.broadcasted_iota(jnp.int32, (t, LANES), 1) < HD_A
    q = q_ref[...] * (HD_A ** -0.5)
    q2 = jnp.concatenate([jnp.where(is_a, q, 0.0), jnp.where(is_a, 0.0, q)], axis=0).astype(BF16)

    s_c = _dot_nt(q2, ck_ref[...].astype(BF16))
    s_n = _dot_nt(q2, kn_ref[...].astype(BF16))
    tq_c = lax.broadcasted_iota(jnp.int32, (2 * t, lb), 0) % t
    kp_c = lax.broadcasted_iota(jnp.int32, (2 * t, lb), 1)
    cnt_c = _pattern_count(lb + tq_c - kp_c)
    tq_n = lax.broadcasted_iota(jnp.int32, (2 * t, t), 0) % t
    kp_n = lax.broadcasted_iota(jnp.int32, (2 * t, t), 1)
    cnt_n = _pattern_count(tq_n - kp_n)

    m = jnp.maximum(jnp.max(jnp.where(cnt_c > 0, s_c, NEG), axis=-1, keepdims=True),
                    jnp.max(jnp.where(cnt_n > 0, s_n, NEG), axis=-1, keepdims=True))
    e_c = jnp.where(cnt_c > 0, jnp.exp(s_c - m), 0.0) * cnt_c.astype(F32)
    e_n = jnp.where(cnt_n > 0, jnp.exp(s_n - m), 0.0) * cnt_n.astype(F32)
    l = jnp.sum(e_c, axis=-1, keepdims=True) + jnp.sum(e_n, axis=-1, keepdims=True)
    o = (_dot(e_c.astype(BF16), cv_ref[...].astype(BF16)) + _dot(e_n.astype(BF16), vn_ref[...].astype(BF16))) / l
    o_ref[...] = jnp.where(is_a, o[:t], o[t:]).astype(o_ref.dtype)


def _attn_sample(qa, ka, va, ck, cv):
    b, t, _ = qa.shape
    lb = ck.shape[1]
    new = pl.BlockSpec((None, t, LANES), lambda i, j: (i, 0, j))
    buf = pl.BlockSpec((None, lb, LANES), lambda i, j: (i, 0, j))
    return pl.pallas_call(
        _attn_sample_kernel,
        grid=(b, D_A // LANES),
        in_specs=[new, new, new, buf, buf],
        out_specs=new,
        out_shape=jax.ShapeDtypeStruct((b, t, D_A), BF16),
        compiler_params=_params("parallel", "parallel"),
        name="attn_sample",
    )(qa, ka, va, ck, cv)


def _gla_kernel(q_ref, k_ref, lf_ref, v_ref, r_ref, gn_ref, st0_ref, o_ref, st_ref,
                qe_scr, ke_scr, qs_scr, kd_scr, dec_scr):
    t = q_ref.shape[0]
    c = GLA_CHUNK
    nc = t // c
    half = c // 2

    b = lf_ref[...]
    in_chunk = lax.broadcasted_iota(jnp.int32, (t, LANES), 0) & (c - 1)
    step = 1
    while step < c:
        b = b + jnp.where(in_chunk >= step, pltpu.roll(b, step, axis=0), 0.0)
        step *= 2
    b3 = b.reshape(nc, c, LANES)
    mid = b3[:, half - 1:half, :]
    last = b3[:, c - 1:c, :]
    q3 = (q_ref[...] * (DK_B ** -0.5)).reshape(nc, c, LANES)
    k3 = k_ref[...].reshape(nc, c, LANES)
    qe_scr[...] = (q3 * jnp.exp(b3 - mid)).reshape(t, LANES).astype(BF16)
    ke_scr[...] = (k3 * jnp.exp(mid - b3)).reshape(t, LANES).astype(BF16)
    qs_scr[...] = (q3 * jnp.exp(b3)).reshape(t, LANES).astype(BF16)
    kd_scr[...] = (k3 * jnp.exp(last - b3)).reshape(t, LANES).astype(BF16)
    dec_scr[...] = jnp.broadcast_to(jnp.exp(last), (nc, 8, LANES))

    causal = (lax.broadcasted_iota(jnp.int32, (c, c), 1) <= lax.broadcasted_iota(jnp.int32, (c, c), 0))
    is_a = lax.broadcasted_iota(jnp.int32, (c, LANES), 1) < DK_B
    gn = gn_ref[...]

    def chunk(i, st):
        r0 = pl.multiple_of(i * c, c)
        rs = pl.ds(r0, c)
        qe, ke, qs, kd = qe_scr[rs, :], ke_scr[rs, :], qs_scr[rs, :], kd_scr[rs, :]
        v = v_ref[rs, :]
        r = r_ref[rs, :]
        st_b = st.astype(BF16)
        upd = None
        for e in range(2):
            sel = is_a if e == 0 else jnp.logical_not(is_a)
            vh = v[:, e * DV_B:(e + 1) * DV_B].astype(BF16)
            att = jnp.where(causal, _dot_nt(jnp.where(sel, qe, 0), ke), 0.0)
            o = _dot(att.astype(BF16), vh) + _dot_nt(jnp.where(sel, qs, 0), st_b)
            u = _dot_tn(vh, jnp.where(sel, kd, 0))
            upd = u if upd is None else upd + u
            rh = r[:, e * DV_B:(e + 1) * DV_B]
            y = _rms(o, gn[:, e * DV_B:(e + 1) * DV_B]) * (rh * jax.nn.sigmoid(rh))
            o_ref[rs, e * DV_B:(e + 1) * DV_B] = y.astype(o_ref.dtype)
        return dec_scr[i][0:1, :] * st + upd

    st_ref[...] = lax.fori_loop(0, nc, chunk, st0_ref[...])


def _gla(qb, kb, lf, vb, rb, g_norm, st0):
    b, t, _ = qb.shape
    assert t % GLA_CHUNK == 0
    qk = pl.BlockSpec((None, t, LANES), lambda i, j: (i, 0, j))
    vr = pl.BlockSpec((None, t, 2 * DV_B), lambda i, j: (i, 0, j))
    st = pl.BlockSpec((None, None, DV_B, LANES), lambda i, j: (i, j, 0, 0))
    return pl.pallas_call(
        _gla_kernel,
        grid=(b, H_B // 2),
        in_specs=[qk, qk, qk, vr, vr, pl.BlockSpec((1, 2 * DV_B), lambda i, j: (0, j)), st],
        out_specs=[vr, st],
        out_shape=[jax.ShapeDtypeStruct((b, t, D_VB), BF16), jax.ShapeDtypeStruct(st0.shape, F32)],
        scratch_shapes=[pltpu.VMEM((t, LANES), BF16)] * 4 + [pltpu.VMEM((t // GLA_CHUNK, 8, LANES), F32)],
        compiler_params=_params("parallel", "parallel"),
        name="gla",
    )(qb, kb, lf, vb, rb, g_norm, st0)


def _state_to_pairs(s):
    b = s.shape[0]
    return s.reshape(b, H_B // 2, 2, DK_B, DV_B).transpose(0, 1, 4, 2, 3).reshape(b, H_B // 2, DV_B, 2 * DK_B)


def _pairs_to_state(sp):
    b = sp.shape[0]
    return sp.reshape(b, H_B // 2, DV_B, 2, DK_B).transpose(0, 1, 3, 4, 2).reshape(b, H_B, DK_B, DV_B)


def _outproj_kernel(oa_ref, ob_ref, x_ref, w_ref, g_ref, x1_ref):
    mixed = _dot(oa_ref[...], w_ref[0:D_A, :]) + _dot(ob_ref[...], w_ref[D_A:, :])
    x1_ref[...] = x_ref[...] + _rms(mixed, g_ref[...])


def _outproj(oa, ob, x2, w_o, g_post, tm):
    n, d = x2.shape
    row = lambda w: pl.BlockSpec((tm, w), lambda i: (i, 0))
    return pl.pallas_call(
        _outproj_kernel,
        grid=(n // tm,),
        in_specs=[row(D_A), row(D_VB), row(d), _const_spec(w_o.shape), _const_spec(g_post.shape)],
        out_specs=row(d),
        out_shape=jax.ShapeDtypeStruct((n, d), F32),
        compiler_params=_params("parallel"),
        name="outproj",
    )(oa, ob, x2, w_o, g_post)


def _gelu(x):
    return 0.5 * x * (1.0 + lax.erf(x * (2.0 ** -0.5)))


def _ffn_chunks(h, wg_ref, wu_ref, wd_ref, cw_ref, cb_ref, fc, shifted, emit_g):
    d_ff = wg_ref.shape[1]
    acc = None
    for ci in range(d_ff // fc):
        sl = slice(ci * fc, (ci + 1) * fc)
        g = _dot(h, wg_ref[:, sl])
        u = _dot(h, wu_ref[:, sl])
        g1, g2 = shifted(g, sl)
        conv = cb_ref[:, sl] + cw_ref[0:1, sl] * g2 + cw_ref[1:2, sl] * g1 + cw_ref[2:3, sl] * g
        part = _dot((_gelu(conv) * u).astype(BF16), wd_ref[sl, :])
        acc = part if acc is None else acc + part
        emit_g(g, sl)
    return acc


def _ffn_prompt_kernel(x_ref, gpre_ref, gpost_ref, wg_ref, wu_ref, wd_ref, cw_ref, cb_ref,
                       out_ref, cst_ref, carry_ref, *, fc):
    tm = x_ref.shape[0]

    @pl.when(pl.program_id(1) == 0)
    def _():
        carry_ref[...] = jnp.zeros_like(carry_ref)

    x1 = x_ref[...]
    h = _rms(x1, gpre_ref[...]).astype(BF16)
    rows = lax.broadcasted_iota(jnp.int32, (tm, fc), 0)

    def shifted(g, sl):
        prev = carry_ref[:, sl]
        g1 = jnp.where(rows == 0, prev[1:2], pltpu.roll(g, 1, axis=0))
        g2 = jnp.where(rows == 0, prev[0:1], jnp.where(rows == 1, prev[1:2], pltpu.roll(g, 2, axis=0)))
        return g1, g2

    def emit_g(g, sl):
        tail = g[tm - 2:tm]
        carry_ref[0:2, sl] = tail
        cst_ref[:, sl] = tail

    f = _ffn_chunks(h, wg_ref, wu_ref, wd_ref, cw_ref, cb_ref, fc, shifted, emit_g)
    out_ref[...] = x1 + _rms(f, gpost_ref[...])


def _ffn_prompt(x1, g_pre, g_post, wg, wu, wd, conv_w, conv_b, tm, fc):
    b, s, d = x1.shape
    d_ff = wg.shape[1]
    tok = pl.BlockSpec((None, tm, d), lambda i, j: (i, j, 0))
    return pl.pallas_call(
        functools.partial(_ffn_prompt_kernel, fc=fc),
        grid=(b, s // tm),
        in_specs=[tok] + [_const_spec(a.shape) for a in (g_pre, g_post, wg, wu, wd, conv_w, conv_b)],
        out_specs=[tok, pl.BlockSpec((None, 2, d_ff), lambda i, j: (i, 0, 0))],
        out_shape=[jax.ShapeDtypeStruct((b, s, d), F32), jax.ShapeDtypeStruct((b, 2, d_ff), F32)],
        scratch_shapes=[pltpu.VMEM((8, d_ff), F32)],
        compiler_params=_params("parallel", "arbitrary"),
        name="ffn_prompt",
    )(x1, g_pre, g_post, wg, wu, wd, conv_w, conv_b)


def _ffn_sample_kernel(x_ref, p1_ref, p2_ref, gpre_ref, gpost_ref, wg_ref, wu_ref, wd_ref, cw_ref, cb_ref,
                       out_ref, g_ref, *, fc, seg):
    tm = x_ref.shape[0]
    x1 = x_ref[...]
    h = _rms(x1, gpre_ref[...]).astype(BF16)
    pos = lax.broadcasted_iota(jnp.int32, (tm, fc), 0) % seg

    def shifted(g, sl):
        g1 = jnp.where(pos >= 1, pltpu.roll(g, 1, axis=0), p1_ref[:, sl])
        g2 = jnp.where(pos >= 2, pltpu.roll(g, 2, axis=0), p2_ref[:, sl])
        return g1, g2

    def emit_g(g, sl):
        g_ref[:, sl] = g

    f = _ffn_chunks(h, wg_ref, wu_ref, wd_ref, cw_ref, cb_ref, fc, shifted, emit_g)
    out_ref[...] = x1 + _rms(f, gpost_ref[...])


def _ffn_sample(x1, conv_state, g_pre, g_post, wg, wu, wd, conv_w, conv_b, fc):
    b, t, d = x1.shape
    d_ff = wg.shape[1]
    assert t >= 2
    zeros = jnp.zeros((b, t - 2, d_ff), F32)
    p1 = jnp.concatenate([conv_state[:, 1:2], zeros, zeros[:, :1]], axis=1).reshape(b * t, d_ff)
    p2 = jnp.concatenate([conv_state, zeros], axis=1).reshape(b * t, d_ff)
    args = (x1.reshape(b * t, d), p1, p2, g_pre, g_post, wg, wu, wd, conv_w, conv_b)
    x2, g = pl.pallas_call(
        functools.partial(_ffn_sample_kernel, fc=fc, seg=t),
        grid=(1,),
        in_specs=[_const_spec(a.shape) for a in args],
        out_specs=[_const_spec((b * t, d)), _const_spec((b * t, d_ff))],
        out_shape=[jax.ShapeDtypeStruct((b * t, d), F32), jax.ShapeDtypeStruct((b * t, d_ff), F32)],
        compiler_params=_params("arbitrary"),
        name="ffn_sample",
    )(*args)
    return x2.reshape(b, t, d), g.reshape(b, t, d_ff)


def _pick_tile(n, want):
    tm = min(n, want)
    assert n % tm == 0
    return tm


def _layer(x, weights, tm_want):
    b, t, d = x.shape
    x2 = x.reshape(b * t, d)
    tm = _pick_tile(b * t, tm_want)
    outs = _inproj(x2, weights["g_pre_mix"], weights["w_main"], weights["w_lr"], weights["w_a2"], weights["b_a"], tm)
    return [o.reshape(b, t, -1) for o in outs]


def kernel(x_prompt, x_sample, cache_k_win, cache_v_win, state_gla, state_ffn_conv,
           w_in, w_a2, b_a, g_gla_norm, w_o, g_pre_mix, g_post_mix, g_pre_ffn, g_post_ffn,
           w_up, conv_w, conv_b, w_down):
    depth = w_in.shape[0]
    assert depth == 1
    l = 0
    d = x_prompt.shape[-1]
    d_ff = w_down.shape[1]
    fc = 256
    assert d_ff % fc == 0

    w_in_b = w_in[l].astype(BF16)
    wts = {
        "w_main": w_in_b[:, :D_MAIN],
        "w_lr": jnp.pad(w_in_b[:, D_MAIN:], ((0, 0), (0, LANES - GATE_RANK))),
        "w_a2": jnp.pad(w_a2[l].astype(BF16), ((0, LANES - GATE_RANK), (0, 0))),
        "b_a": b_a[l].reshape(1, -1),
        "g_pre_mix": g_pre_mix[l].reshape(1, -1),
    }
    g_norm = g_gla_norm[l].reshape(1, -1)
    w_o_b = w_o[l].astype(BF16)
    g_post_mix_r = g_post_mix[l].reshape(1, -1)
    g_pre_ffn_r = g_pre_ffn[l].reshape(1, -1)
    g_post_ffn_r = g_post_ffn[l].reshape(1, -1)
    w_up_b = w_up[l].astype(BF16)
    wg, wu = w_up_b[:, :d_ff], w_up_b[:, d_ff:]
    wd = w_down[l].astype(BF16)
    cw = conv_w[l]
    cb = conv_b[l].reshape(1, -1)

    bp, sp, _ = x_prompt.shape
    qa, ka, va, qb, kb, vb, rb, lf = _layer(x_prompt, wts, 512)
    oa = _attn_prompt(qa, ka, va)
    st0 = jnp.zeros((bp, H_B // 2, DV_B, 2 * DK_B), F32)
    ob, st = _gla(qb, kb, lf, vb, rb, g_norm, st0)
    tm = _pick_tile(bp * sp, 512)
    x1 = _outproj(oa.reshape(bp * sp, -1), ob.reshape(bp * sp, -1), x_prompt.reshape(bp * sp, d),
                  w_o_b, g_post_mix_r, tm)
    y_prompt, conv_p = _ffn_prompt(x1.reshape(bp, sp, d), g_pre_ffn_r, g_post_ffn_r, wg, wu, wd, cw, cb,
                                   _pick_tile(sp, 512), fc)
    buf = min(max(w for w, _ in PATTERNS), sp)
    win_k_p = ka[:, sp - buf:].reshape(1, bp, buf, H_A, HD_A)
    win_v_p = va[:, sp - buf:].reshape(1, bp, buf, H_A, HD_A)
    gla_p = _pairs_to_state(st)[None]
    conv_p = conv_p[None]

    bs, ts, _ = x_sample.shape
    qa, ka, va, qb, kb, vb, rb, lf = _layer(x_sample, wts, 256)
    lbuf = cache_k_win.shape[2]
    oa = _attn_sample(qa, ka, va, cache_k_win[l].reshape(bs, lbuf, D_A), cache_v_win[l].reshape(bs, lbuf, D_A))
    tpad = -(-ts // GLA_CHUNK) * GLA_CHUNK
    padt = lambda a: jnp.pad(a, ((0, 0), (0, tpad - ts), (0, 0)))
    ob, st = _gla(padt(qb), padt(kb), padt(lf), padt(vb), padt(rb), g_norm, _state_to_pairs(state_gla[l]))
    ob = ob[:, :ts]
    x1 = _outproj(oa.reshape(bs * ts, -1), ob.reshape(bs * ts, -1), x_sample.reshape(bs * ts, d),
                  w_o_b, g_post_mix_r, _pick_tile(bs * ts, 256))
    y_sample, g_s = _ffn_sample(x1.reshape(bs, ts, d), state_ffn_conv[l], g_pre_ffn_r, g_post_ffn_r,
                                wg, wu, wd, cw, cb, fc)
    win_k_s = ka.reshape(1, bs, ts, H_A, HD_A)
    win_v_s = va.reshape(1, bs, ts, H_A, HD_A)
    gla_s = _pairs_to_state(st)[None]
    conv_s = g_s[:, ts - 2:][None]

    return (y_prompt, y_sample, win_k_p, win_v_p, gla_p, conv_p, win_k_s, win_v_s, gla_s, conv_s)
```

```python
import functools

import jax
import jax.numpy as jnp
from jax import lax
from jax.experimental import pallas as pl
from jax.experimental.pallas import tpu as pltpu

F32 = jnp.float32
BF16 = jnp.bfloat16

EPS = 1e-6
NEG = -1e30
HD_A = 64
H_A = 8
H_B = 4
DK_B = 64
DV_B = 128
GATE_RANK = 16
GLA_TAU = 16.0
PATTERNS = ((128, 1), (512, 4), (2048, 16))
BAND = 128
GLA_CHUNK = 32
LANES = 128
VMEM_LIMIT = 56 * 1024 * 1024

D_A = H_A * HD_A
D_QKB = H_B * DK_B
D_VB = H_B * DV_B
D_MAIN = 3 * D_A + 2 * D_QKB + 2 * D_VB


def _dot(a, b):
    return jnp.dot(a, b, preferred_element_type=F32)


def _dot_nt(a, b):
    return lax.dot_general(a, b, (((1,), (1,)), ((), ())), preferred_element_type=F32)


def _dot_tn(a, b):
    return lax.dot_general(a, b, (((0,), (0,)), ((), ())), preferred_element_type=F32)


def _rms(x, g):
    return x * lax.rsqrt(jnp.mean(x * x, axis=-1, keepdims=True) + EPS) * g


def _params(*sem):
    return pltpu.CompilerParams(dimension_semantics=sem, vmem_limit_bytes=VMEM_LIMIT)


def _const_spec(shape):
    nd = len(shape)
    return pl.BlockSpec(shape, lambda *_: (0,) * nd, pipeline_mode=pl.Buffered(1))


def _inproj_kernel(x_ref, g_ref, w_ref, wlr_ref, wa2_ref, ba_ref,
                   qa_ref, ka_ref, va_ref, qb_ref, kb_ref, vb_ref, rb_ref, lf_ref):
    xn = _rms(x_ref[...], g_ref[...]).astype(BF16)
    off = 0
    for ref in (qa_ref, ka_ref, va_ref, qb_ref, kb_ref, vb_ref, rb_ref):
        n = ref.shape[-1]
        ref[...] = _dot(xn, w_ref[:, off:off + n])
        off += n
    a_lr = _dot(xn, wlr_ref[...])
    z = _dot(a_lr.astype(BF16), wa2_ref[...]) + ba_ref[...]
    lf_ref[...] = (jnp.minimum(z, 0.0) - jnp.log1p(jnp.exp(-jnp.abs(z)))) * (1.0 / GLA_TAU)


def _inproj(x2, g_pre, w_main, w_lr, w_a2, b_a, tm):
    n, d = x2.shape
    widths = (D_A, D_A, D_A, D_QKB, D_QKB, D_VB, D_VB, D_QKB)
    row = lambda w: pl.BlockSpec((tm, w), lambda i: (i, 0))
    return pl.pallas_call(
        _inproj_kernel,
        grid=(n // tm,),
        in_specs=[row(d), _const_spec(g_pre.shape), _const_spec(w_main.shape), _const_spec(w_lr.shape),
                  _const_spec(w_a2.shape), _const_spec(b_a.shape)],
        out_specs=[row(w) for w in widths],
        out_shape=[jax.ShapeDtypeStruct((n, w), F32) for w in widths],
        compiler_params=_params("parallel"),
        name="inproj",
    )(x2, g_pre, w_main, w_lr, w_a2, b_a)


class _AttnUnit:
    def __init__(self, q_ref, k_ref, v_ref, dil, start, prev):
        n = BAND
        self.prev = prev

        def rows(s):
            if dil == 1:
                return pl.ds(pl.multiple_of(s, n), n)
            return pl.ds(s, n, stride=dil)

        self.rows = rows(start)
        is_a = lax.broadcasted_iota(jnp.int32, (n, LANES), 1) < HD_A
        q = q_ref[self.rows, :] * (HD_A ** -0.5)
        q2 = jnp.concatenate([jnp.where(is_a, q, 0.0), jnp.where(is_a, 0.0, q)], axis=0).astype(BF16)
        k = [k_ref[self.rows, :].astype(BF16)]
        v = [v_ref[self.rows, :].astype(BF16)]
        if prev is not None:
            rows_prev = rows(start - n * dil if prev is True else jnp.where(prev, start - n * dil, start))
            k.append(k_ref[rows_prev, :].astype(BF16))
            v.append(v_ref[rows_prev, :].astype(BF16))
        self.scores = _dot_nt(q2, jnp.concatenate(k, axis=0))
        ones_a = is_a.astype(BF16)
        zero = jnp.zeros((), BF16)
        self.values = jnp.concatenate(
            [jnp.concatenate([jnp.where(is_a, vb, zero), ones_a], axis=1) for vb in v]
            + [jnp.concatenate([jnp.where(is_a, zero, vb), 1 - ones_a], axis=1) for vb in v], axis=0)

    def softmax(self):
        n = BAND
        row = lax.broadcasted_iota(jnp.int32, (2 * n, n), 0) & (n - 1)
        col = lax.broadcasted_iota(jnp.int32, (2 * n, n), 1)
        s = self.scores
        s_own = jnp.where(col <= row, s[:, :n], NEG)
        m = s_own
        if self.prev is not None:
            reach = row if self.prev is True else row + jnp.where(self.prev, 0, n)
            s_prev = jnp.where(col >= reach, s[:, n:], NEG)
            m = jnp.maximum(s_own, s_prev)
        m = jnp.max(m, axis=-1, keepdims=True)
        p = [jnp.exp(s_own - m).astype(BF16)]
        if self.prev is not None:
            p.append(jnp.exp(s_prev - m).astype(BF16))
        self.probs = jnp.concatenate([pb[:n] for pb in p] + [pb[n:] for pb in p], axis=1)
        is_a = lax.broadcasted_iota(jnp.int32, (n, LANES), 1) < HD_A
        self.m = jnp.where(is_a, m[:n], m[n:])

    def weighted_values(self):
        out = _dot(self.probs, self.values)
        return self.rows, self.m, out[:, LANES:], out[:, :LANES]


def _attn_prompt_kernel(q_ref, k_ref, v_ref, o_ref, m_scr, l_scr, acc_scr, *, group):
    seq = q_ref.shape[0]
    n = BAND
    for p, (_, dil) in enumerate(PATTERNS):
        nb = seq // (n * dil)
        assert (dil * nb) % group == 0 and (group % nb == 0 or nb % group == 0)

        def units(i, carry, dil=dil, nb=nb, p=p):
            todo = []
            for j in range(group):
                if group % nb == 0:
                    r = i * (group // nb) + j // nb
                    c = j % nb
                    prev = None if nb == 1 else c > 0
                    if prev is False:
                        prev = None
                else:
                    r = (i * group) // nb
                    c = (i * group) % nb + j
                    prev = True if j > 0 else c > 0
                start = r + c * (n * dil)
                todo.append(_AttnUnit(q_ref, k_ref, v_ref, dil, start, prev))
            for unit in todo:
                unit.softmax()
            done = [unit.weighted_values() for unit in todo]
            for sl, m2, l2, o2 in done:
                if p == 0:
                    m_scr[sl, :] = m2
                    l_scr[sl, :] = l2
                    acc_scr[sl, :] = o2
                else:
                    m0 = m_scr[sl, :]
                    mn = jnp.maximum(m0, m2)
                    a0 = jnp.exp(m0 - mn)
                    a2 = jnp.exp(m2 - mn)
                    m_scr[sl, :] = mn
                    l_scr[sl, :] = a0 * l_scr[sl, :] + a2 * l2
                    acc_scr[sl, :] = a0 * acc_scr[sl, :] + a2 * o2
            return carry

        lax.fori_loop(0, dil * nb // group, units, 0)

    o_ref[...] = (acc_scr[...] / l_scr[...]).astype(o_ref.dtype)


def _attn_prompt(qa, ka, va, group=8):
    b, s, _ = qa.shape
    assert s % (BAND * max(d for _, d in PATTERNS)) == 0
    blk = pl.BlockSpec((None, s, LANES), lambda i, j: (i, 0, j))
    return pl.pallas_call(
        functools.partial(_attn_prompt_kernel, group=group),
        grid=(b, D_A // LANES),
        in_specs=[blk, blk, blk],
        out_specs=blk,
        out_shape=jax.ShapeDtypeStruct((b, s, D_A), BF16),
        scratch_shapes=[pltpu.VMEM((s, LANES), F32)] * 3,
        compiler_params=_params("parallel", "parallel"),
        name="attn_prompt",
    )(qa, ka, va)


def _pattern_count(dist):
    cnt = jnp.zeros(dist.shape, jnp.int32)
    for _, dil in PATTERNS:
        shift = dil.bit_length() - 1
        hit = jnp.logical_and((dist & (dil - 1)) == 0, (dist >> shift) <= BAND)
        cnt = cnt + hit.astype(jnp.int32)
    return jnp.where(dist >= 0, cnt, 0)


def _attn_sample_kernel(q_ref, kn_ref, vn_ref, ck_ref, cv_ref, o_ref):
    t = q_ref.shape[0]
    lb = ck_ref.shape[0]
    is_a = lax.broadcasted_iota(jnp.int32, (t, LANES), 1) < HD_A
    q = q_ref[...] * (HD_A ** -0.5)
    q2 = jnp.concatenate([jnp.where(is_a, q, 0.0), jnp.where(is_a, 0.0, q)], axis=0).astype(BF16)

    s_c = _dot_nt(q2, ck_ref[...].astype(BF16))
    s_n = _dot_nt(q2, kn_ref[...].astype(BF16))
    tq_c = lax.broadcasted_iota(jnp.int32, (2 * t, lb), 0) % t
    kp_c = lax.broadcasted_iota(jnp.int32, (2 * t, lb), 1)
    cnt_c = _pattern_count(lb + tq_c - kp_c)
    tq_n = lax.broadcasted_iota(jnp.int32, (2 * t, t), 0) % t
    kp_n = lax.broadcasted_iota(jnp.int32, (2 * t, t), 1)
    cnt_n = _pattern_count(tq_n - kp_n)

    m = jnp.maximum(jnp.max(jnp.where(cnt_c > 0, s_c, NEG), axis=-1, keepdims=True),
                    jnp.max(jnp.where(cnt_n > 0, s_n, NEG), axis=-1, keepdims=True))
    e_c = jnp.where(cnt_c > 0, jnp.exp(s_c - m), 0.0) * cnt_c.astype(F32)
    e_n = jnp.where(cnt_n > 0, jnp.exp(s_n - m), 0.0) * cnt_n.astype(F32)
    l = jnp.sum(e_c, axis=-1, keepdims=True) + jnp.sum(e_n, axis=-1, keepdims=True)
    o = (_dot(e_c.astype(BF16), cv_ref[...].astype(BF16)) + _dot(e_n.astype(BF16), vn_ref[...].astype(BF16))) / l
    o_ref[...] = jnp.where(is_a, o[:t], o[t:]).astype(o_ref.dtype)


def _attn_sample(qa, ka, va, ck, cv):
    b, t, _ = qa.shape
    lb = ck.shape[1]
    new = pl.BlockSpec((None, t, LANES), lambda i, j: (i, 0, j))
    buf = pl.BlockSpec((None, lb, LANES), lambda i, j: (i, 0, j))
    return pl.pallas_call(
        _attn_sample_kernel,
        grid=(b, D_A // LANES),
        in_specs=[new, new, new, buf, buf],
        out_specs=new,
        out_shape=jax.ShapeDtypeStruct((b, t, D_A), BF16),
        compiler_params=_params("parallel", "parallel"),
        name="attn_sample",
    )(qa, ka, va, ck, cv)


def _gla_kernel(q_ref, k_ref, lf_ref, v_ref, r_ref, gn_ref, st0_ref, o_ref, st_ref,
                qe_scr, ke_scr, qs_scr, kd_scr, dec_scr):
    t = q_ref.shape[0]
    c = GLA_CHUNK
    nc = t // c
    half = c // 2

    b = lf_ref[...]
    in_chunk = lax.broadcasted_iota(jnp.int32, (t, LANES), 0) & (c - 1)
    step = 1
    while step < c:
        b = b + jnp.where(in_chunk >= step, pltpu.roll(b, step, axis=0), 0.0)
        step *= 2
    b3 = b.reshape(nc, c, LANES)
    mid = b3[:, half - 1:half, :]
    last = b3[:, c - 1:c, :]
    q3 = (q_ref[...] * (DK_B ** -0.5)).reshape(nc, c, LANES)
    k3 = k_ref[...].reshape(nc, c, LANES)
    qe_scr[...] = (q3 * jnp.exp(b3 - mid)).reshape(t, LANES).astype(BF16)
    ke_scr[...] = (k3 * jnp.exp(mid - b3)).reshape(t, LANES).astype(BF16)
    qs_scr[...] = (q3 * jnp.exp(b3)).reshape(t, LANES).astype(BF16)
    kd_scr[...] = (k3 * jnp.exp(last - b3)).reshape(t, LANES).astype(BF16)
    dec_scr[...] = jnp.broadcast_to(jnp.exp(last), (nc, 8, LANES))

    causal = (lax.broadcasted_iota(jnp.int32, (c, c), 1) <= lax.broadcasted_iota(jnp.int32, (c, c), 0))
    is_a = lax.broadcasted_iota(jnp.int32, (c, LANES), 1) < DK_B
    gn = gn_ref[...]

    def chunk(i, st):
        r0 = pl.multiple_of(i * c, c)
        rs = pl.ds(r0, c)
        qe, ke, qs, kd = qe_scr[rs, :], ke_scr[rs, :], qs_scr[rs, :], kd_scr[rs, :]
        v = v_ref[rs, :]
        r = r_ref[rs, :]
        st_b = st.astype(BF16)
        upd = None
        for e in range(2):
            sel = is_a if e == 0 else jnp.logical_not(is_a)
            vh = v[:, e * DV_B:(e + 1) * DV_B].astype(BF16)
            att = jnp.where(causal, _dot_nt(jnp.where(sel, qe, 0), ke), 0.0)
            o = _dot(att.astype(BF16), vh) + _dot_nt(jnp.where(sel, qs, 0), st_b)
            u = _dot_tn(vh, jnp.where(sel, kd, 0))
            upd = u if upd is None else upd + u
            rh = r[:, e * DV_B:(e + 1) * DV_B]
            y = _rms(o, gn[:, e * DV_B:(e + 1) * DV_B]) * (rh * jax.nn.sigmoid(rh))
            o_ref[rs, e * DV_B:(e + 1) * DV_B] = y.astype(o_ref.dtype)
        return dec_scr[i][0:1, :] * st + upd

    st_ref[...] = lax.fori_loop(0, nc, chunk, st0_ref[...])


def _gla(qb, kb, lf, vb, rb, g_norm, st0):
    b, t, _ = qb.shape
    assert t % GLA_CHUNK == 0
    qk = pl.BlockSpec((None, t, LANES), lambda i, j: (i, 0, j))
    vr = pl.BlockSpec((None, t, 2 * DV_B), lambda i, j: (i, 0, j))
    st = pl.BlockSpec((None, None, DV_B, LANES), lambda i, j: (i, j, 0, 0))
    return pl.pallas_call(
        _gla_kernel,
        grid=(b, H_B // 2),
        in_specs=[qk, qk, qk, vr, vr, pl.BlockSpec((1, 2 * DV_B), lambda i, j: (0, j)), st],
        out_specs=[vr, st],
        out_shape=[jax.ShapeDtypeStruct((b, t, D_VB), BF16), jax.ShapeDtypeStruct(st0.shape, F32)],
        scratch_shapes=[pltpu.VMEM((t, LANES), BF16)] * 4 + [pltpu.VMEM((t // GLA_CHUNK, 8, LANES), F32)],
        compiler_params=_params("parallel", "parallel"),
        name="gla",
    )(qb, kb, lf, vb, rb, g_norm, st0)


def _state_to_pairs(s):
    b = s.shape[0]
    return s.reshape(b, H_B // 2, 2, DK_B, DV_B).transpose(0, 1, 4, 2, 3).reshape(b, H_B // 2, DV_B, 2 * DK_B)


def _pairs_to_state(sp):
    b = sp.shape[0]
    return sp.reshape(b, H_B // 2, DV_B, 2, DK_B).transpose(0, 1, 3, 4, 2).reshape(b, H_B, DK_B, DV_B)


def _outproj_kernel(oa_ref, ob_ref, x_ref, w_ref, g_ref, x1_ref):
    mixed = _dot(oa_ref[...], w_ref[0:D_A, :]) + _dot(ob_ref[...], w_ref[D_A:, :])
    x1_ref[...] = x_ref[...] + _rms(mixed, g_ref[...])


def _outproj(oa, ob, x2, w_o, g_post, tm):
    n, d = x2.shape
    row = lambda w: pl.BlockSpec((tm, w), lambda i: (i, 0))
    return pl.pallas_call(
        _outproj_kernel,
        grid=(n // tm,),
        in_specs=[row(D_A), row(D_VB), row(d), _const_spec(w_o.shape), _const_spec(g_post.shape)],
        out_specs=row(d),
        out_shape=jax.ShapeDtypeStruct((n, d), F32),
        compiler_params=_params("parallel"),
        name="outproj",
    )(oa, ob, x2, w_o, g_post)


def _gelu(x):
    return 0.5 * x * (1.0 + lax.erf(x * (2.0 ** -0.5)))


def _ffn_chunks(h, wg_ref, wu_ref, wd_ref, cw_ref, cb_ref, fc, shifted, emit_g):
    d_ff = wg_ref.shape[1]
    acc = None
    for ci in range(d_ff // fc):
        sl = slice(ci * fc, (ci + 1) * fc)
        g = _dot(h, wg_ref[:, sl])
        u = _dot(h, wu_ref[:, sl])
        g1, g2 = shifted(g, sl)
        conv = cb_ref[:, sl] + cw_ref[0:1, sl] * g2 + cw_ref[1:2, sl] * g1 + cw_ref[2:3, sl] * g
        part = _dot((_gelu(conv) * u).astype(BF16), wd_ref[sl, :])
        acc = part if acc is None else acc + part
        emit_g(g, sl)
    return acc


def _ffn_prompt_kernel(x_ref, gpre_ref, gpost_ref, wg_ref, wu_ref, wd_ref, cw_ref, cb_ref,
                       out_ref, cst_ref, carry_ref, *, fc):
    tm = x_ref.shape[0]

    @pl.when(pl.program_id(1) == 0)
    def _():
        carry_ref[...] = jnp.zeros_like(carry_ref)

    x1 = x_ref[...]
    h = _rms(x1, gpre_ref[...]).astype(BF16)
    rows = lax.broadcasted_iota(jnp.int32, (tm, fc), 0)

    def shifted(g, sl):
        prev = carry_ref[:, sl]
        g1 = jnp.where(rows == 0, prev[1:2], pltpu.roll(g, 1, axis=0))
        g2 = jnp.where(rows == 0, prev[0:1], jnp.where(rows == 1, prev[1:2], pltpu.roll(g, 2, axis=0)))
        return g1, g2

    def emit_g(g, sl):
        tail = g[tm - 2:tm]
        carry_ref[0:2, sl] = tail
        cst_ref[:, sl] = tail

    f = _ffn_chunks(h, wg_ref, wu_ref, wd_ref, cw_ref, cb_ref, fc, shifted, emit_g)
    out_ref[...] = x1 + _rms(f, gpost_ref[...])


def _ffn_prompt(x1, g_pre, g_post, wg, wu, wd, conv_w, conv_b, tm, fc):
    b, s, d = x1.shape
    d_ff = wg.shape[1]
    tok = pl.BlockSpec((None, tm, d), lambda i, j: (i, j, 0))
    return pl.pallas_call(
        functools.partial(_ffn_prompt_kernel, fc=fc),
        grid=(b, s // tm),
        in_specs=[tok] + [_const_spec(a.shape) for a in (g_pre, g_post, wg, wu, wd, conv_w, conv_b)],
        out_specs=[tok, pl.BlockSpec((None, 2, d_ff), lambda i, j: (i, 0, 0))],
        out_shape=[jax.ShapeDtypeStruct((b, s, d), F32), jax.ShapeDtypeStruct((b, 2, d_ff), F32)],
        scratch_shapes=[pltpu.VMEM((8, d_ff), F32)],
        compiler_params=_params("parallel", "arbitrary"),
        name="ffn_prompt",
    )(x1, g_pre, g_post, wg, wu, wd, conv_w, conv_b)


def _ffn_sample_kernel(x_ref, p1_ref, p2_ref, gpre_ref, gpost_ref, wg_ref, wu_ref, wd_ref, cw_ref, cb_ref,
                       out_ref, g_ref, *, fc, seg):
    tm = x_ref.shape[0]
    x1 = x_ref[...]
    h = _rms(x1, gpre_ref[...]).astype(BF16)
    pos = lax.broadcasted_iota(jnp.int32, (tm, fc), 0) % seg

    def shifted(g, sl):
        g1 = jnp.where(pos >= 1, pltpu.roll(g, 1, axis=0), p1_ref[:, sl])
        g2 = jnp.where(pos >= 2, pltpu.roll(g, 2, axis=0), p2_ref[:, sl])
        return g1, g2

    def emit_g(g, sl):
        g_ref[:, sl] = g

    f = _ffn_chunks(h, wg_ref, wu_ref, wd_ref, cw_ref, cb_ref, fc, shifted, emit_g)
    out_ref[...] = x1 + _rms(f, gpost_ref[...])


def _ffn_sample(x1, conv_state, g_pre, g_post, wg, wu, wd, conv_w, conv_b, fc):
    b, t, d = x1.shape
    d_ff = wg.shape[1]
    assert t >= 2
    zeros = jnp.zeros((b, t - 2, d_ff), F32)
    p1 = jnp.concatenate([conv_state[:, 1:2], zeros, zeros[:, :1]], axis=1).reshape(b * t, d_ff)
    p2 = jnp.concatenate([conv_state, zeros], axis=1).reshape(b * t, d_ff)
    args = (x1.reshape(b * t, d), p1, p2, g_pre, g_post, wg, wu, wd, conv_w, conv_b)
    x2, g = pl.pallas_call(
        functools.partial(_ffn_sample_kernel, fc=fc, seg=t),
        grid=(1,),
        in_specs=[_const_spec(a.shape) for a in args],
        out_specs=[_const_spec((b * t, d)), _const_spec((b * t, d_ff))],
        out_shape=[jax.ShapeDtypeStruct((b * t, d), F32), jax.ShapeDtypeStruct((b * t, d_ff), F32)],
        compiler_params=_params("arbitrary"),
        name="ffn_sample",
    )(*args)
    return x2.reshape(b, t, d), g.reshape(b, t, d_ff)


def _pick_tile(n, want):
    tm = min(n, want)
    assert n % tm == 0
    return tm


def _layer(x, weights, tm_want):
    b, t, d = x.shape
    x2 = x.reshape(b * t, d)
    tm = _pick_tile(b * t, tm_want)
    outs = _inproj(x2, weights["g_pre_mix"], weights["w_main"], weights["w_lr"], weights["w_a2"], weights["b_a"], tm)
    return [o.reshape(b, t, -1) for o in outs]


def kernel(x_prompt, x_sample, cache_k_win, cache_v_win, state_gla, state_ffn_conv,
           w_in, w_a2, b_a, g_gla_norm, w_o, g_pre_mix, g_post_mix, g_pre_ffn, g_post_ffn,
           w_up, conv_w, conv_b, w_down):
    depth = w_in.shape[0]
    assert depth == 1
    l = 0
    d = x_prompt.shape[-1]
    d_ff = w_down.shape[1]
    fc = 256
    assert d_ff % fc == 0

    w_in_b = w_in[l].astype(BF16)
    wts = {
        "w_main": w_in_b[:, :D_MAIN],
        "w_lr": jnp.pad(w_in_b[:, D_MAIN:], ((0, 0), (0, LANES - GATE_RANK))),
        "w_a2": jnp.pad(w_a2[l].astype(BF16), ((0, LANES - GATE_RANK), (0, 0))),
        "b_a": b_a[l].reshape(1, -1),
        "g_pre_mix": g_pre_mix[l].reshape(1, -1),
    }
    g_norm = g_gla_norm[l].reshape(1, -1)
    w_o_b = w_o[l].astype(BF16)
    g_post_mix_r = g_post_mix[l].reshape(1, -1)
    g_pre_ffn_r = g_pre_ffn[l].reshape(1, -1)
    g_post_ffn_r = g_post_ffn[l].reshape(1, -1)
    w_up_b = w_up[l].astype(BF16)
    wg, wu = w_up_b[:, :d_ff], w_up_b[:, d_ff:]
    wd = w_down[l].astype(BF16)
    cw = conv_w[l]
    cb = conv_b[l].reshape(1, -1)

    bp, sp, _ = x_prompt.shape
    qa, ka, va, qb, kb, vb, rb, lf = _layer(x_prompt, wts, 512)
    oa = _attn_prompt(qa, ka, va)
    st0 = jnp.zeros((bp, H_B // 2, DV_B, 2 * DK_B), F32)
    ob, st = _gla(qb, kb, lf, vb, rb, g_norm, st0)
    tm = _pick_tile(bp * sp, 512)
    x1 = _outproj(oa.reshape(bp * sp, -1), ob.reshape(bp * sp, -1), x_prompt.reshape(bp * sp, d),
                  w_o_b, g_post_mix_r, tm)
    y_prompt, conv_p = _ffn_prompt(x1.reshape(bp, sp, d), g_pre_ffn_r, g_post_ffn_r, wg, wu, wd, cw, cb,
                                   _pick_tile(sp, 512), fc)
    buf = min(max(w for w, _ in PATTERNS), sp)
    win_k_p = ka[:, sp - buf:].reshape(1, bp, buf, H_A, HD_A)
    win_v_p = va[:, sp - buf:].reshape(1, bp, buf, H_A, HD_A)
    gla_p = _pairs_to_state(st)[None]
    conv_p = conv_p[None]

    bs, ts, _ = x_sample.shape
    qa, ka, va, qb, kb, vb, rb, lf = _layer(x_sample, wts, 256)
    lbuf = cache_k_win.shape[2]
    oa = _attn_sample(qa, ka, va, cache_k_win[l].reshape(bs, lbuf, D_A), cache_v_win[l].reshape(bs, lbuf, D_A))
    tpad = -(-ts // GLA_CHUNK) * GLA_CHUNK
    padt = lambda a: jnp.pad(a, ((0, 0), (0, tpad - ts), (0, 0)))
    ob, st = _gla(padt(qb), padt(kb), padt(lf), padt(vb), padt(rb), g_norm, _state_to_pairs(state_gla[l]))
    ob = ob[:, :ts]
    x1 = _outproj(oa.reshape(bs * ts, -1), ob.reshape(bs * ts, -1), x_sample.reshape(bs * ts, d),
                  w_o_b, g_post_mix_r, _pick_tile(bs * ts, 256))
    y_sample, g_s = _ffn_sample(x1.reshape(bs, ts, d), state_ffn_conv[l], g_pre_ffn_r, g_post_ffn_r,
                                wg, wu, wd, cw, cb, fc)
    win_k_s = ka.reshape(1, bs, ts, H_A, HD_A)
    win_v_s = va.reshape(1, bs, ts, H_A, HD_A)
    gla_s = _pairs_to_state(st)[None]
    conv_s = g_s[:, ts - 2:][None]

    return (y_prompt, y_sample, win_k_p, win_v_p, gla_p, conv_p, win_k_s, win_v_s, gla_s, conv_s)
```

```python
import functools

import jax
import jax.numpy as jnp
from jax import lax
from jax.experimental import pallas as pl
from jax.experimental.pallas import tpu as pltpu

F32 = jnp.float32
BF16 = jnp.bfloat16

EPS = 1e-6
NEG = -1e30
HD_A = 64
H_A = 8
H_B = 4
DK_B = 64
DV_B = 128
GATE_RANK = 16
GLA_TAU = 16.0
PATTERNS = ((128, 1), (512, 4), (2048, 16))
BAND = 128
GLA_CHUNK = 32
LANES = 128
VMEM_LIMIT = 56 * 1024 * 1024

D_A = H_A * HD_A
D_QKB = H_B * DK_B
D_VB = H_B * DV_B
D_MAIN = 3 * D_A + 2 * D_QKB + 2 * D_VB


def _dot(a, b):
    return jnp.dot(a, b, preferred_element_type=F32)


def _dot_nt(a, b):
    return lax.dot_general(a, b, (((1,), (1,)), ((), ())), preferred_element_type=F32)


def _dot_tn(a, b):
    return lax.dot_general(a, b, (((0,), (0,)), ((), ())), preferred_element_type=F32)


def _rms(x, g):
    return x * lax.rsqrt(jnp.mean(x * x, axis=-1, keepdims=True) + EPS) * g


def _params(*sem):
    return pltpu.CompilerParams(dimension_semantics=sem, vmem_limit_bytes=VMEM_LIMIT)


def _const_spec(shape):
    nd = len(shape)
    return pl.BlockSpec(shape, lambda *_: (0,) * nd, pipeline_mode=pl.Buffered(1))


def _inproj_kernel(x_ref, g_ref, w_ref, wlr_ref, wa2_ref, ba_ref,
                   qa_ref, ka_ref, va_ref, qb_ref, kb_ref, vb_ref, rb_ref, lf_ref):
    xn = _rms(x_ref[...], g_ref[...]).astype(BF16)
    off = 0
    for ref in (qa_ref, ka_ref, va_ref, qb_ref, kb_ref, vb_ref, rb_ref):
        n = ref.shape[-1]
        ref[...] = _dot(xn, w_ref[:, off:off + n])
        off += n
    a_lr = _dot(xn, wlr_ref[...])
    z = _dot(a_lr.astype(BF16), wa2_ref[...]) + ba_ref[...]
    lf_ref[...] = (jnp.minimum(z, 0.0) - jnp.log1p(jnp.exp(-jnp.abs(z)))) * (1.0 / GLA_TAU)


def _inproj(x2, g_pre, w_main, w_lr, w_a2, b_a, tm):
    n, d = x2.shape
    widths = (D_A, D_A, D_A, D_QKB, D_QKB, D_VB, D_VB, D_QKB)
    row = lambda w: pl.BlockSpec((tm, w), lambda i: (i, 0))
    return pl.pallas_call(
        _inproj_kernel,
        grid=(n // tm,),
        in_specs=[row(d), _const_spec(g_pre.shape), _const_spec(w_main.shape), _const_spec(w_lr.shape),
                  _const_spec(w_a2.shape), _const_spec(b_a.shape)],
        out_specs=[row(w) for w in widths],
        out_shape=[jax.ShapeDtypeStruct((n, w), F32) for w in widths],
        compiler_params=_params("parallel"),
        name="inproj",
    )(x2, g_pre, w_main, w_lr, w_a2, b_a)


class _AttnUnit:
    def __init__(self, q_ref, k_ref, v_ref, dil, start, prev):
        n = BAND
        self.prev = prev

        def rows(s):
            if dil == 1:
                return pl.ds(pl.multiple_of(s, n), n)
            return pl.ds(s, n, stride=dil)

        self.rows = rows(start)
        is_a = lax.broadcasted_iota(jnp.int32, (n, LANES), 1) < HD_A
        q = q_ref[self.rows, :] * (HD_A ** -0.5)
        q2 = jnp.concatenate([jnp.where(is_a, q, 0.0), jnp.where(is_a, 0.0, q)], axis=0).astype(BF16)
        k = [k_ref[self.rows, :].astype(BF16)]
        v = [v_ref[self.rows, :].astype(BF16)]
        if prev is not None:
            rows_prev = rows(start - n * dil if prev is True else jnp.where(prev, start - n * dil, start))
            k.append(k_ref[rows_prev, :].astype(BF16))
            v.append(v_ref[rows_prev, :].astype(BF16))
        self.scores = _dot_nt(q2, jnp.concatenate(k, axis=0))
        ones_a = is_a.astype(BF16)
        zero = jnp.zeros((), BF16)
        self.values = jnp.concatenate(
            [jnp.concatenate([jnp.where(is_a, vb, zero), ones_a], axis=1) for vb in v]
            + [jnp.concatenate([jnp.where(is_a, zero, vb), 1 - ones_a], axis=1) for vb in v], axis=0)

    def softmax(self):
        n = BAND
        row = lax.broadcasted_iota(jnp.int32, (2 * n, n), 0) & (n - 1)
        col = lax.broadcasted_iota(jnp.int32, (2 * n, n), 1)
        s = self.scores
        s_own = jnp.where(col <= row, s[:, :n], NEG)
        m = s_own
        if self.prev is not None:
            reach = row if self.prev is True else row + jnp.where(self.prev, 0, n)
            s_prev = jnp.where(col >= reach, s[:, n:], NEG)
            m = jnp.maximum(s_own, s_prev)
        m = jnp.max(m, axis=-1, keepdims=True)
        p = [jnp.exp(s_own - m).astype(BF16)]
        if self.prev is not None:
            p.append(jnp.exp(s_prev - m).astype(BF16))
        self.probs = jnp.concatenate([pb[:n] for pb in p] + [pb[n:] for pb in p], axis=1)
        is_a = lax.broadcasted_iota(jnp.int32, (n, LANES), 1) < HD_A
        self.m = jnp.where(is_a, m[:n], m[n:])

    def weighted_values(self):
        out = _dot(self.probs, self.values)
        return self.rows, self.m, out[:, LANES:], out[:, :LANES]


def _attn_prompt_kernel(q_ref, k_ref, v_ref, o_ref, m_scr, l_scr, acc_scr, *, group):
    seq = q_ref.shape[0]
    n = BAND
    for p, (_, dil) in enumerate(PATTERNS):
        nb = seq // (n * dil)
        assert (dil * nb) % group == 0 and (group % nb == 0 or nb % group == 0)

        def units(i, carry, dil=dil, nb=nb, p=p):
            todo = []
            for j in range(group):
                if group % nb == 0:
                    r = i * (group // nb) + j // nb
                    c = j % nb
                    prev = None if nb == 1 else c > 0
                    if prev is False:
                        prev = None
                else:
                    r = (i * group) // nb
                    c = (i * group) % nb + j
                    prev = True if j > 0 else c > 0
                start = r + c * (n * dil)
                todo.append(_AttnUnit(q_ref, k_ref, v_ref, dil, start, prev))
            for unit in todo:
                unit.softmax()
            done = [unit.weighted_values() for unit in todo]
            for sl, m2, l2, o2 in done:
                if p == 0:
                    m_scr[sl, :] = m2
                    l_scr[sl, :] = l2
                    acc_scr[sl, :] = o2
                else:
                    m0 = m_scr[sl, :]
                    mn = jnp.maximum(m0, m2)
                    a0 = jnp.exp(m0 - mn)
                    a2 = jnp.exp(m2 - mn)
                    m_scr[sl, :] = mn
                    l_scr[sl, :] = a0 * l_scr[sl, :] + a2 * l2
                    acc_scr[sl, :] = a0 * acc_scr[sl, :] + a2 * o2
            return carry

        lax.fori_loop(0, dil * nb // group, units, 0)

    o_ref[...] = (acc_scr[...] / l_scr[...]).astype(o_ref.dtype)


def _attn_prompt(qa, ka, va, group=8):
    b, s, _ = qa.shape
    assert s % (BAND * max(d for _, d in PATTERNS)) == 0
    blk = pl.BlockSpec((None, s, LANES), lambda i, j: (i, 0, j))
    return pl.pallas_call(
        functools.partial(_attn_prompt_kernel, group=group),
        grid=(b, D_A // LANES),
        in_specs=[blk, blk, blk],
        out_specs=blk,
        out_shape=jax.ShapeDtypeStruct((b, s, D_A), BF16),
        scratch_shapes=[pltpu.VMEM((s, LANES), F32)] * 3,
        compiler_params=_params("parallel", "parallel"),
        name="attn_prompt",
    )(qa, ka, va)


def _pattern_count(dist):
    cnt = jnp.zeros(dist.shape, jnp.int32)
    for _, dil in PATTERNS:
        shift = dil.bit_length() - 1
        hit = jnp.logical_and((dist & (dil - 1)) == 0, (dist >> shift) <= BAND)
        cnt = cnt + hit.astype(jnp.int32)
    return jnp.where(dist >= 0, cnt, 0)


def _attn_sample_kernel(q_ref, kn_ref, vn_ref, ck_ref, cv_ref, o_ref):
    t = q_ref.shape[0]
    lb = ck_ref.shape[0]
    is_a = lax.broadcasted_iota(jnp.int32, (t, LANES), 1) < HD_A
    q = q_ref[...] * (HD_A ** -0.5)
    q2 = jnp.concatenate([jnp.where(is_a, q, 0.0), jnp.where(is_a, 0.0, q)], axis=0).astype(BF16)

    s_c = _dot_nt(q2, ck_ref[...].astype(BF16))
    s_n = _dot_nt(q2, kn_ref[...].astype(BF16))
    tq_c = lax.broadcasted_iota(jnp.int32, (2 * t, lb), 0) % t
    kp_c = lax.broadcasted_iota(jnp.int32, (2 * t, lb), 1)
    cnt_c = _pattern_count(lb + tq_c - kp_c)
    tq_n = lax.broadcasted_iota(jnp.int32, (2 * t, t), 0) % t
    kp_n = lax.broadcasted_iota(jnp.int32, (2 * t, t), 1)
    cnt_n = _pattern_count(tq_n - kp_n)

    m = jnp.maximum(jnp.max(jnp.where(cnt_c > 0, s_c, NEG), axis=-1, keepdims=True),
                    jnp.max(jnp.where(cnt_n > 0, s_n, NEG), axis=-1, keepdims=True))
    e_c = jnp.where(cnt_c > 0, jnp.exp(s_c - m), 0.0) * cnt_c.astype(F32)
    e_n = jnp.where(cnt_n > 0, jnp.exp(s_n - m), 0.0) * cnt_n.astype(F32)
    l = jnp.sum(e_c, axis=-1, keepdims=True) + jnp.sum(e_n, axis=-1, keepdims=True)
    o = (_dot(e_c.astype(BF16), cv_ref[...].astype(BF16)) + _dot(e_n.astype(BF16), vn_ref[...].astype(BF16))) / l
    o_ref[...] = jnp.where(is_a, o[:t], o[t:]).astype(o_ref.dtype)


def _attn_sample(qa, ka, va, ck, cv):
    b, t, _ = qa.shape
    lb = ck.shape[1]
    new = pl.BlockSpec((None, t, LANES), lambda i, j: (i, 0, j))
    buf = pl.BlockSpec((None, lb, LANES), lambda i, j: (i, 0, j))
    return pl.pallas_call(
        _attn_sample_kernel,
        grid=(b, D_A // LANES),
        in_specs=[new, new, new, buf, buf],
        out_specs=new,
        out_shape=jax.ShapeDtypeStruct((b, t, D_A), BF16),
        compiler_params=_params("parallel", "parallel"),
        name="attn_sample",
    )(qa, ka, va, ck, cv)


def _gla_kernel(q_ref, k_ref, lf_ref, v_ref, r_ref, gn_ref, st0_ref, o_ref, st_ref,
                qe_scr, ke_scr, qs_scr, kd_scr, dec_scr, *, group):
    t = q_ref.shape[0]
    c = GLA_CHUNK
    nc = t // c
    half = c // 2

    b = lf_ref[...]
    in_chunk = lax.broadcasted_iota(jnp.int32, (t, LANES), 0) & (c - 1)
    step = 1
    while step < c:
        b = b + jnp.where(in_chunk >= step, pltpu.roll(b, step, axis=0), 0.0)
        step *= 2
    b3 = b.reshape(nc, c, LANES)
    mid = b3[:, half - 1:half, :]
    last = b3[:, c - 1:c, :]
    q3 = (q_ref[...] * (DK_B ** -0.5)).reshape(nc, c, LANES)
    k3 = k_ref[...].reshape(nc, c, LANES)
    qe_scr[...] = (q3 * jnp.exp(b3 - mid)).reshape(t, LANES).astype(BF16)
    ke_scr[...] = (k3 * jnp.exp(mid - b3)).reshape(t, LANES).astype(BF16)
    qs_scr[...] = (q3 * jnp.exp(b3)).reshape(t, LANES).astype(BF16)
    kd_scr[...] = (k3 * jnp.exp(last - b3)).reshape(t, LANES).astype(BF16)
    dec_scr[...] = jnp.broadcast_to(jnp.exp(last), (nc, 8, LANES))

    causal2 = (lax.broadcasted_iota(jnp.int32, (2 * c, c), 1)
               <= (lax.broadcasted_iota(jnp.int32, (2 * c, c), 0) & (c - 1)))
    is_a = lax.broadcasted_iota(jnp.int32, (c, LANES), 1) < DK_B
    gn = gn_ref[...]

    def both_heads(x):
        return jnp.concatenate([jnp.where(is_a, x, 0), jnp.where(is_a, 0, x)], axis=0)

    def chunks(i, st):
        base = i * group
        rows = [pl.ds(pl.multiple_of((base + j) * c, c), c) for j in range(group)]
        vals, att, upd = [], [], []
        for rs in rows:
            v = v_ref[rs, :]
            vals.append((v[:, :DV_B].astype(BF16), v[:, DV_B:].astype(BF16)))
            att.append(_dot_nt(both_heads(qe_scr[rs, :]), ke_scr[rs, :]))
            upd.append(_dot_tn(jnp.concatenate(vals[-1], axis=0), both_heads(kd_scr[rs, :])))
        intra = []
        for (va, vb), a in zip(vals, att):
            a = jnp.where(causal2, a, 0.0).astype(BF16)
            intra.append(jnp.concatenate([_dot(a[:c], va), _dot(a[c:], vb)], axis=1))
        before = []
        for j in range(group):
            before.append(st.astype(BF16))
            st = dec_scr[base + j][0:1, :] * st + upd[j]
        for j, rs in enumerate(rows):
            inter = _dot_nt(both_heads(qs_scr[rs, :]), before[j])
            o = intra[j] + jnp.concatenate([inter[:c], inter[c:]], axis=1)
            r = r_ref[rs, :]
            for e in range(2):
                cols = slice(e * DV_B, (e + 1) * DV_B)
                y = _rms(o[:, cols], gn[:, cols]) * (r[:, cols] * jax.nn.sigmoid(r[:, cols]))
                o_ref[rs, cols] = y.astype(o_ref.dtype)
        return st

    st_ref[...] = lax.fori_loop(0, nc // group, chunks, st0_ref[...])


def _gla(qb, kb, lf, vb, rb, g_norm, st0, group):
    b, t, _ = qb.shape
    assert t % (GLA_CHUNK * group) == 0
    qk = pl.BlockSpec((None, t, LANES), lambda i, j: (i, 0, j))
    vr = pl.BlockSpec((None, t, 2 * DV_B), lambda i, j: (i, 0, j))
    st = pl.BlockSpec((None, None, DV_B, LANES), lambda i, j: (i, j, 0, 0))
    return pl.pallas_call(
        functools.partial(_gla_kernel, group=group),
        grid=(b, H_B // 2),
        in_specs=[qk, qk, qk, vr, vr, pl.BlockSpec((1, 2 * DV_B), lambda i, j: (0, j)), st],
        out_specs=[vr, st],
        out_shape=[jax.ShapeDtypeStruct((b, t, D_VB), BF16), jax.ShapeDtypeStruct(st0.shape, F32)],
        scratch_shapes=[pltpu.VMEM((t, LANES), BF16)] * 4 + [pltpu.VMEM((t // GLA_CHUNK, 8, LANES), F32)],
        compiler_params=_params("parallel", "parallel"),
        name="gla",
    )(qb, kb, lf, vb, rb, g_norm, st0)


def _state_to_pairs(s):
    b = s.shape[0]
    return s.reshape(b, H_B // 2, 2, DK_B, DV_B).transpose(0, 1, 4, 2, 3).reshape(b, H_B // 2, DV_B, 2 * DK_B)


def _pairs_to_state(sp):
    b = sp.shape[0]
    return sp.reshape(b, H_B // 2, DV_B, 2, DK_B).transpose(0, 1, 3, 4, 2).reshape(b, H_B, DK_B, DV_B)


def _outproj_kernel(oa_ref, ob_ref, x_ref, w_ref, g_ref, x1_ref):
    mixed = _dot(oa_ref[...], w_ref[0:D_A, :]) + _dot(ob_ref[...], w_ref[D_A:, :])
    x1_ref[...] = x_ref[...] + _rms(mixed, g_ref[...])


def _outproj(oa, ob, x2, w_o, g_post, tm):
    n, d = x2.shape
    row = lambda w: pl.BlockSpec((tm, w), lambda i: (i, 0))
    return pl.pallas_call(
        _outproj_kernel,
        grid=(n // tm,),
        in_specs=[row(D_A), row(D_VB), row(d), _const_spec(w_o.shape), _const_spec(g_post.shape)],
        out_specs=row(d),
        out_shape=jax.ShapeDtypeStruct((n, d), F32),
        compiler_params=_params("parallel"),
        name="outproj",
    )(oa, ob, x2, w_o, g_post)


def _gelu(x):
    return 0.5 * x * (1.0 + lax.erf(x * (2.0 ** -0.5)))


def _ffn_chunks(h, wg_ref, wu_ref, wd_ref, cw_ref, cb_ref, act_ref, fc, shifted, emit_g):
    d_ff = wg_ref.shape[1]
    for ci in range(d_ff // fc):
        sl = slice(ci * fc, (ci + 1) * fc)
        g = _dot(h, wg_ref[:, sl])
        u = _dot(h, wu_ref[:, sl])
        g1, g2 = shifted(g, sl)
        conv = cb_ref[:, sl] + cw_ref[0:1, sl] * g2 + cw_ref[1:2, sl] * g1 + cw_ref[2:3, sl] * g
        act_ref[:, sl] = (_gelu(conv) * u).astype(BF16)
        emit_g(g, sl)
    return _dot(act_ref[...], wd_ref[...])


def _ffn_prompt_kernel(x_ref, gpre_ref, gpost_ref, wg_ref, wu_ref, wd_ref, cw_ref, cb_ref,
                       out_ref, cst_ref, carry_ref, act_ref, *, fc):
    tm = x_ref.shape[0]

    @pl.when(pl.program_id(1) == 0)
    def _():
        carry_ref[...] = jnp.zeros_like(carry_ref)

    x1 = x_ref[...]
    h = _rms(x1, gpre_ref[...]).astype(BF16)
    rows = lax.broadcasted_iota(jnp.int32, (tm, fc), 0)

    def shifted(g, sl):
        prev = carry_ref[:, sl]
        g1 = jnp.where(rows == 0, prev[1:2], pltpu.roll(g, 1, axis=0))
        g2 = jnp.where(rows == 0, prev[0:1], jnp.where(rows == 1, prev[1:2], pltpu.roll(g, 2, axis=0)))
        return g1, g2

    def emit_g(g, sl):
        tail = g[tm - 2:tm]
        carry_ref[0:2, sl] = tail
        cst_ref[:, sl] = tail

    f = _ffn_chunks(h, wg_ref, wu_ref, wd_ref, cw_ref, cb_ref, act_ref, fc, shifted, emit_g)
    out_ref[...] = x1 + _rms(f, gpost_ref[...])


def _ffn_prompt(x1, g_pre, g_post, wg, wu, wd, conv_w, conv_b, tm, fc):
    b, s, d = x1.shape
    d_ff = wg.shape[1]
    tok = pl.BlockSpec((None, tm, d), lambda i, j: (i, j, 0))
    return pl.pallas_call(
        functools.partial(_ffn_prompt_kernel, fc=fc),
        grid=(b, s // tm),
        in_specs=[tok] + [_const_spec(a.shape) for a in (g_pre, g_post, wg, wu, wd, conv_w, conv_b)],
        out_specs=[tok, pl.BlockSpec((None, 2, d_ff), lambda i, j: (i, 0, 0))],
        out_shape=[jax.ShapeDtypeStruct((b, s, d), F32), jax.ShapeDtypeStruct((b, 2, d_ff), F32)],
        scratch_shapes=[pltpu.VMEM((8, d_ff), F32), pltpu.VMEM((tm, d_ff), BF16)],
        compiler_params=_params("parallel", "arbitrary"),
        name="ffn_prompt",
    )(x1, g_pre, g_post, wg, wu, wd, conv_w, conv_b)


def _ffn_sample_kernel(x_ref, p1_ref, p2_ref, gpre_ref, gpost_ref, wg_ref, wu_ref, wd_ref, cw_ref, cb_ref,
                       out_ref, g_ref, act_ref, *, fc, seg):
    tm = x_ref.shape[0]
    x1 = x_ref[...]
    h = _rms(x1, gpre_ref[...]).astype(BF16)
    pos = lax.broadcasted_iota(jnp.int32, (tm, fc), 0) % seg

    def shifted(g, sl):
        g1 = jnp.where(pos >= 1, pltpu.roll(g, 1, axis=0), p1_ref[:, sl])
        g2 = jnp.where(pos >= 2, pltpu.roll(g, 2, axis=0), p2_ref[:, sl])
        return g1, g2

    def emit_g(g, sl):
        g_ref[:, sl] = g

    f = _ffn_chunks(h, wg_ref, wu_ref, wd_ref, cw_ref, cb_ref, act_ref, fc, shifted, emit_g)
    out_ref[...] = x1 + _rms(f, gpost_ref[...])


def _ffn_sample(x1, conv_state, g_pre, g_post, wg, wu, wd, conv_w, conv_b, fc):
    b, t, d = x1.shape
    d_ff = wg.shape[1]
    assert t >= 2
    zeros = jnp.zeros((b, t - 2, d_ff), F32)
    p1 = jnp.concatenate([conv_state[:, 1:2], zeros, zeros[:, :1]], axis=1).reshape(b * t, d_ff)
    p2 = jnp.concatenate([conv_state, zeros], axis=1).reshape(b * t, d_ff)
    args = (x1.reshape(b * t, d), p1, p2, g_pre, g_post, wg, wu, wd, conv_w, conv_b)
    x2, g = pl.pallas_call(
        functools.partial(_ffn_sample_kernel, fc=fc, seg=t),
        grid=(1,),
        in_specs=[_const_spec(a.shape) for a in args],
        out_specs=[_const_spec((b * t, d)), _const_spec((b * t, d_ff))],
        out_shape=[jax.ShapeDtypeStruct((b * t, d), F32), jax.ShapeDtypeStruct((b * t, d_ff), F32)],
        scratch_shapes=[pltpu.VMEM((b * t, d_ff), BF16)],
        compiler_params=_params("arbitrary"),
        name="ffn_sample",
    )(*args)
    return x2.reshape(b, t, d), g.reshape(b, t, d_ff)


def _pick_tile(n, want):
    tm = min(n, want)
    assert n % tm == 0
    return tm


def _layer(x, weights, tm_want):
    b, t, d = x.shape
    x2 = x.reshape(b * t, d)
    tm = _pick_tile(b * t, tm_want)
    outs = _inproj(x2, weights["g_pre_mix"], weights["w_main"], weights["w_lr"], weights["w_a2"], weights["b_a"], tm)
    return [o.reshape(b, t, -1) for o in outs]


def kernel(x_prompt, x_sample, cache_k_win, cache_v_win, state_gla, state_ffn_conv,
           w_in, w_a2, b_a, g_gla_norm, w_o, g_pre_mix, g_post_mix, g_pre_ffn, g_post_ffn,
           w_up, conv_w, conv_b, w_down):
    depth = w_in.shape[0]
    assert depth == 1
    l = 0
    d = x_prompt.shape[-1]
    d_ff = w_down.shape[1]
    fc = 256
    assert d_ff % fc == 0

    w_in_b = w_in[l].astype(BF16)
    wts = {
        "w_main": w_in_b[:, :D_MAIN],
        "w_lr": jnp.pad(w_in_b[:, D_MAIN:], ((0, 0), (0, LANES - GATE_RANK))),
        "w_a2": jnp.pad(w_a2[l].astype(BF16), ((0, LANES - GATE_RANK), (0, 0))),
        "b_a": b_a[l].reshape(1, -1),
        "g_pre_mix": g_pre_mix[l].reshape(1, -1),
    }
    g_norm = g_gla_norm[l].reshape(1, -1)
    w_o_b = w_o[l].astype(BF16)
    g_post_mix_r = g_post_mix[l].reshape(1, -1)
    g_pre_ffn_r = g_pre_ffn[l].reshape(1, -1)
    g_post_ffn_r = g_post_ffn[l].reshape(1, -1)
    w_up_b = w_up[l].astype(BF16)
    wg, wu = w_up_b[:, :d_ff], w_up_b[:, d_ff:]
    wd = w_down[l].astype(BF16)
    cw = conv_w[l]
    cb = conv_b[l].reshape(1, -1)

    bp, sp, _ = x_prompt.shape
    qa, ka, va, qb, kb, vb, rb, lf = _layer(x_prompt, wts, 512)
    oa = _attn_prompt(qa, ka, va)
    st0 = jnp.zeros((bp, H_B // 2, DV_B, 2 * DK_B), F32)
    ob, st = _gla(qb, kb, lf, vb, rb, g_norm, st0, group=16)
    tm = _pick_tile(bp * sp, 512)
    x1 = _outproj(oa.reshape(bp * sp, -1), ob.reshape(bp * sp, -1), x_prompt.reshape(bp * sp, d),
                  w_o_b, g_post_mix_r, tm)
    y_prompt, conv_p = _ffn_prompt(x1.reshape(bp, sp, d), g_pre_ffn_r, g_post_ffn_r, wg, wu, wd, cw, cb,
                                   _pick_tile(sp, 512), fc)
    buf = min(max(w for w, _ in PATTERNS), sp)
    win_k_p = ka[:, sp - buf:].reshape(1, bp, buf, H_A, HD_A)
    win_v_p = va[:, sp - buf:].reshape(1, bp, buf, H_A, HD_A)
    gla_p = _pairs_to_state(st)[None]
    conv_p = conv_p[None]

    bs, ts, _ = x_sample.shape
    qa, ka, va, qb, kb, vb, rb, lf = _layer(x_sample, wts, 256)
    lbuf = cache_k_win.shape[2]
    oa = _attn_sample(qa, ka, va, cache_k_win[l].reshape(bs, lbuf, D_A), cache_v_win[l].reshape(bs, lbuf, D_A))
    tpad = -(-ts // GLA_CHUNK) * GLA_CHUNK
    padt = lambda a: jnp.pad(a, ((0, 0), (0, tpad - ts), (0, 0)))
    ob, st = _gla(padt(qb), padt(kb), padt(lf), padt(vb), padt(rb), g_norm, _state_to_pairs(state_gla[l]),
                  group=1)
    ob = ob[:, :ts]
    x1 = _outproj(oa.reshape(bs * ts, -1), ob.reshape(bs * ts, -1), x_sample.reshape(bs * ts, d),
                  w_o_b, g_post_mix_r, _pick_tile(bs * ts, 256))
    y_sample, g_s = _ffn_sample(x1.reshape(bs, ts, d), state_ffn_conv[l], g_pre_ffn_r, g_post_ffn_r,
                                wg, wu, wd, cw, cb, fc)
    win_k_s = ka.reshape(1, bs, ts, H_A, HD_A)
    win_v_s = va.reshape(1, bs, ts, H_A, HD_A)
    gla_s = _pairs_to_state(st)[None]
    conv_s = g_s[:, ts - 2:][None]

    return (y_prompt, y_sample, win_k_p, win_v_p, gla_p, conv_p, win_k_s, win_v_s, gla_s, conv_s)
```

```python
import functools

import jax
import jax.numpy as jnp
from jax import lax
from jax.experimental import pallas as pl
from jax.experimental.pallas import tpu as pltpu

F32 = jnp.float32
BF16 = jnp.bfloat16

EPS = 1e-6
NEG = -1e30
HD_A = 64
H_A = 8
H_B = 4
DK_B = 64
DV_B = 128
GATE_RANK = 16
GLA_TAU = 16.0
PATTERNS = ((128, 1), (512, 4), (2048, 16))
BAND = 128
GLA_CHUNK = 32
LANES = 128
VMEM_LIMIT = 56 * 1024 * 1024

D_A = H_A * HD_A
D_QKB = H_B * DK_B
D_VB = H_B * DV_B
D_MAIN = 3 * D_A + 2 * D_QKB + 2 * D_VB


def _dot(a, b):
    return jnp.dot(a, b, preferred_element_type=F32)


def _dot_nt(a, b):
    return lax.dot_general(a, b, (((1,), (1,)), ((), ())), preferred_element_type=F32)


def _dot_tn(a, b):
    return lax.dot_general(a, b, (((0,), (0,)), ((), ())), preferred_element_type=F32)


def _rms(x, g):
    return x * lax.rsqrt(jnp.mean(x * x, axis=-1, keepdims=True) + EPS) * g


def _params(*sem):
    return pltpu.CompilerParams(dimension_semantics=sem, vmem_limit_bytes=VMEM_LIMIT)


def _const_spec(shape):
    nd = len(shape)
    return pl.BlockSpec(shape, lambda *_: (0,) * nd, pipeline_mode=pl.Buffered(1))


def _inproj_kernel(x_ref, g_ref, w_ref, wlr_ref, wa2_ref, ba_ref,
                   qa_ref, ka_ref, va_ref, qb_ref, kb_ref, vb_ref, rb_ref, lf_ref, *kv_t_refs):
    xn = _rms(x_ref[...], g_ref[...]).astype(BF16)
    off = 0
    for ref in (qa_ref, ka_ref, va_ref, qb_ref, kb_ref, vb_ref, rb_ref):
        n = ref.shape[-1]
        ref[...] = _dot(xn, w_ref[:, off:off + n])
        off += n
    if kv_t_refs:
        kt_ref, vt_ref = kv_t_refs
        kt_ref[...] = ka_ref[...].T
        vt_ref[...] = va_ref[...].T
    a_lr = _dot(xn, wlr_ref[...])
    z = _dot(a_lr.astype(BF16), wa2_ref[...]) + ba_ref[...]
    lf_ref[...] = (jnp.minimum(z, 0.0) - jnp.log1p(jnp.exp(-jnp.abs(z)))) * (1.0 / GLA_TAU)


def _inproj(x2, g_pre, w_main, w_lr, w_a2, b_a, tm, seq=None):
    n, d = x2.shape
    widths = (D_A, D_A, D_A, D_QKB, D_QKB, D_VB, D_VB, D_QKB)
    row = lambda w: pl.BlockSpec((tm, w), lambda i: (i, 0))
    out_specs = [row(w) for w in widths]
    out_shape = [jax.ShapeDtypeStruct((n, w), F32) for w in widths]
    if seq is not None:
        per = seq // tm
        out_specs += [pl.BlockSpec((None, D_A, tm), lambda i: (i // per, 0, i % per))] * 2
        out_shape += [jax.ShapeDtypeStruct((n // seq, D_A, seq), F32)] * 2
    return pl.pallas_call(
        _inproj_kernel,
        grid=(n // tm,),
        in_specs=[row(d), _const_spec(g_pre.shape), _const_spec(w_main.shape), _const_spec(w_lr.shape),
                  _const_spec(w_a2.shape), _const_spec(b_a.shape)],
        out_specs=out_specs,
        out_shape=out_shape,
        compiler_params=_params("parallel"),
        name="inproj",
    )(x2, g_pre, w_main, w_lr, w_a2, b_a)


class _AttnUnit:
    def __init__(self, q_ref, k_ref, v_ref, dil, start, prev):
        n = BAND
        self.prev = prev

        def rows(s):
            if dil == 1:
                return pl.ds(pl.multiple_of(s, n), n)
            return pl.ds(s, n, stride=dil)

        self.rows = rows(start)
        is_a = lax.broadcasted_iota(jnp.int32, (n, LANES), 1) < HD_A
        q = q_ref[self.rows, :] * (HD_A ** -0.5)
        q2 = jnp.concatenate([jnp.where(is_a, q, 0.0), jnp.where(is_a, 0.0, q)], axis=0).astype(BF16)
        k = [k_ref[self.rows, :].astype(BF16)]
        v = [v_ref[self.rows, :].astype(BF16)]
        if prev is not None:
            rows_prev = rows(start - n * dil if prev is True else jnp.where(prev, start - n * dil, start))
            k.append(k_ref[rows_prev, :].astype(BF16))
            v.append(v_ref[rows_prev, :].astype(BF16))
        self.scores = _dot_nt(q2, jnp.concatenate(k, axis=0))
        ones_a = is_a.astype(BF16)
        zero = jnp.zeros((), BF16)
        self.values = jnp.concatenate(
            [jnp.concatenate([jnp.where(is_a, vb, zero), ones_a], axis=1) for vb in v]
            + [jnp.concatenate([jnp.where(is_a, zero, vb), 1 - ones_a], axis=1) for vb in v], axis=0)

    def softmax(self):
        n = BAND
        row = lax.broadcasted_iota(jnp.int32, (2 * n, n), 0) & (n - 1)
        col = lax.broadcasted_iota(jnp.int32, (2 * n, n), 1)
        s = self.scores
        s_own = jnp.where(col <= row, s[:, :n], NEG)
        m = s_own
        if self.prev is not None:
            reach = row if self.prev is True else row + jnp.where(self.prev, 0, n)
            s_prev = jnp.where(col >= reach, s[:, n:], NEG)
            m = jnp.maximum(s_own, s_prev)
        m = jnp.max(m, axis=-1, keepdims=True)
        p = [jnp.exp(s_own - m).astype(BF16)]
        if self.prev is not None:
            p.append(jnp.exp(s_prev - m).astype(BF16))
        self.probs = jnp.concatenate([pb[:n] for pb in p] + [pb[n:] for pb in p], axis=1)
        is_a = lax.broadcasted_iota(jnp.int32, (n, LANES), 1) < HD_A
        self.m = jnp.where(is_a, m[:n], m[n:])

    def weighted_values(self):
        out = _dot(self.probs, self.values)
        return self.rows, self.m, out[:, LANES:], out[:, :LANES]


def _attn_prompt_kernel(q_ref, k_ref, v_ref, o_ref, m_scr, l_scr, acc_scr, *, group):
    seq = q_ref.shape[0]
    n = BAND
    for p, (_, dil) in enumerate(PATTERNS):
        nb = seq // (n * dil)
        assert (dil * nb) % group == 0 and (group % nb == 0 or nb % group == 0)

        def units(i, carry, dil=dil, nb=nb, p=p):
            todo = []
            for j in range(group):
                if group % nb == 0:
                    r = i * (group // nb) + j // nb
                    c = j % nb
                    prev = None if nb == 1 else c > 0
                    if prev is False:
                        prev = None
                else:
                    r = (i * group) // nb
                    c = (i * group) % nb + j
                    prev = True if j > 0 else c > 0
                start = r + c * (n * dil)
                todo.append(_AttnUnit(q_ref, k_ref, v_ref, dil, start, prev))
            for unit in todo:
                unit.softmax()
            done = [unit.weighted_values() for unit in todo]
            for sl, m2, l2, o2 in done:
                if p == 0:
                    m_scr[sl, :] = m2
                    l_scr[sl, :] = l2
                    acc_scr[sl, :] = o2
                else:
                    m0 = m_scr[sl, :]
                    mn = jnp.maximum(m0, m2)
                    a0 = jnp.exp(m0 - mn)
                    a2 = jnp.exp(m2 - mn)
                    m_scr[sl, :] = mn
                    l_scr[sl, :] = a0 * l_scr[sl, :] + a2 * l2
                    acc_scr[sl, :] = a0 * acc_scr[sl, :] + a2 * o2
            return carry

        lax.fori_loop(0, dil * nb // group, units, 0)

    o_ref[...] = (acc_scr[...] / l_scr[...]).astype(o_ref.dtype)


def _attn_prompt(qa, ka, va, group=8):
    b, s, _ = qa.shape
    assert s % (BAND * max(d for _, d in PATTERNS)) == 0
    blk = pl.BlockSpec((None, s, LANES), lambda i, j: (i, 0, j))
    return pl.pallas_call(
        functools.partial(_attn_prompt_kernel, group=group),
        grid=(b, D_A // LANES),
        in_specs=[blk, blk, blk],
        out_specs=blk,
        out_shape=jax.ShapeDtypeStruct((b, s, D_A), BF16),
        scratch_shapes=[pltpu.VMEM((s, LANES), F32)] * 3,
        compiler_params=_params("parallel", "parallel"),
        name="attn_prompt",
    )(qa, ka, va)


def _pattern_count(dist):
    cnt = jnp.zeros(dist.shape, jnp.int32)
    for _, dil in PATTERNS:
        shift = dil.bit_length() - 1
        hit = jnp.logical_and((dist & (dil - 1)) == 0, (dist >> shift) <= BAND)
        cnt = cnt + hit.astype(jnp.int32)
    return jnp.where(dist >= 0, cnt, 0)


def _attn_sample_kernel(q_ref, kn_ref, vn_ref, ck_ref, cv_ref, o_ref):
    t = q_ref.shape[0]
    lb = ck_ref.shape[1]
    is_a = lax.broadcasted_iota(jnp.int32, (t, LANES), 1) < HD_A
    q = q_ref[...] * (HD_A ** -0.5)
    q2 = jnp.concatenate([jnp.where(is_a, q, 0.0), jnp.where(is_a, 0.0, q)], axis=0).astype(BF16)

    s_c = _dot(q2, ck_ref[...].astype(BF16))
    s_n = _dot_nt(q2, kn_ref[...].astype(BF16))
    tq_c = lax.broadcasted_iota(jnp.int32, (2 * t, lb), 0) % t
    kp_c = lax.broadcasted_iota(jnp.int32, (2 * t, lb), 1)
    cnt_c = _pattern_count(lb + tq_c - kp_c)
    tq_n = lax.broadcasted_iota(jnp.int32, (2 * t, t), 0) % t
    kp_n = lax.broadcasted_iota(jnp.int32, (2 * t, t), 1)
    cnt_n = _pattern_count(tq_n - kp_n)

    m = jnp.maximum(jnp.max(jnp.where(cnt_c > 0, s_c, NEG), axis=-1, keepdims=True),
                    jnp.max(jnp.where(cnt_n > 0, s_n, NEG), axis=-1, keepdims=True))
    e_c = jnp.where(cnt_c > 0, jnp.exp(s_c - m), 0.0) * cnt_c.astype(F32)
    e_n = jnp.where(cnt_n > 0, jnp.exp(s_n - m), 0.0) * cnt_n.astype(F32)
    l = jnp.sum(e_c, axis=-1, keepdims=True) + jnp.sum(e_n, axis=-1, keepdims=True)
    o = (_dot_nt(e_c.astype(BF16), cv_ref[...].astype(BF16)) + _dot(e_n.astype(BF16), vn_ref[...].astype(BF16))) / l
    o_ref[...] = jnp.where(is_a, o[:t], o[t:]).astype(o_ref.dtype)


def _attn_sample(qa, ka, va, ck, cv):
    b, t, _ = qa.shape
    lb = ck.shape[2]
    new = pl.BlockSpec((None, t, LANES), lambda i, j: (i, 0, j))
    buf = pl.BlockSpec((None, LANES, lb), lambda i, j: (i, j, 0))
    return pl.pallas_call(
        _attn_sample_kernel,
        grid=(b, D_A // LANES),
        in_specs=[new, new, new, buf, buf],
        out_specs=new,
        out_shape=jax.ShapeDtypeStruct((b, t, D_A), BF16),
        compiler_params=_params("parallel", "parallel"),
        name="attn_sample",
    )(qa, ka, va, ck, cv)


def _gla_kernel(q_ref, k_ref, lf_ref, v_ref, r_ref, gn_ref, st0_ref, o_ref, st_ref,
                qe_scr, ke_scr, qs_scr, kd_scr, dec_scr, *, group):
    t = q_ref.shape[0]
    c = GLA_CHUNK
    nc = t // c
    half = c // 2

    b = lf_ref[...]
    in_chunk = lax.broadcasted_iota(jnp.int32, (t, LANES), 0) & (c - 1)
    step = 1
    while step < c:
        b = b + jnp.where(in_chunk >= step, pltpu.roll(b, step, axis=0), 0.0)
        step *= 2
    b3 = b.reshape(nc, c, LANES)
    mid = b3[:, half - 1:half, :]
    last = b3[:, c - 1:c, :]
    q3 = (q_ref[...] * (DK_B ** -0.5)).reshape(nc, c, LANES)
    k3 = k_ref[...].reshape(nc, c, LANES)
    qe_scr[...] = (q3 * jnp.exp(b3 - mid)).reshape(t, LANES).astype(BF16)
    ke_scr[...] = (k3 * jnp.exp(mid - b3)).reshape(t, LANES).astype(BF16)
    qs_scr[...] = (q3 * jnp.exp(b3)).reshape(t, LANES).astype(BF16)
    kd_scr[...] = (k3 * jnp.exp(last - b3)).reshape(t, LANES).astype(BF16)
    dec_scr[...] = jnp.broadcast_to(jnp.exp(last), (nc, 8, LANES))

    causal2 = (lax.broadcasted_iota(jnp.int32, (2 * c, c), 1)
               <= (lax.broadcasted_iota(jnp.int32, (2 * c, c), 0) & (c - 1)))
    is_a = lax.broadcasted_iota(jnp.int32, (c, LANES), 1) < DK_B
    gn = gn_ref[...]

    def both_heads(x):
        return jnp.concatenate([jnp.where(is_a, x, 0), jnp.where(is_a, 0, x)], axis=0)

    def chunks(i, st):
        base = i * group
        rows = [pl.ds(pl.multiple_of((base + j) * c, c), c) for j in range(group)]
        vals, att, upd = [], [], []
        for rs in rows:
            v = v_ref[rs, :]
            vals.append((v[:, :DV_B].astype(BF16), v[:, DV_B:].astype(BF16)))
            att.append(_dot_nt(both_heads(qe_scr[rs, :]), ke_scr[rs, :]))
            upd.append(_dot_tn(jnp.concatenate(vals[-1], axis=0), both_heads(kd_scr[rs, :])))
        intra = []
        for (va, vb), a in zip(vals, att):
            a = jnp.where(causal2, a, 0.0).astype(BF16)
            intra.append(jnp.concatenate([_dot(a[:c], va), _dot(a[c:], vb)], axis=1))
        before = []
        for j in range(group):
            before.append(st.astype(BF16))
            st = dec_scr[base + j][0:1, :] * st + upd[j]
        for j, rs in enumerate(rows):
            inter = _dot_nt(both_heads(qs_scr[rs, :]), before[j])
            o = intra[j] + jnp.concatenate([inter[:c], inter[c:]], axis=1)
            r = r_ref[rs, :]
            for e in range(2):
                cols = slice(e * DV_B, (e + 1) * DV_B)
                y = _rms(o[:, cols], gn[:, cols]) * (r[:, cols] * jax.nn.sigmoid(r[:, cols]))
                o_ref[rs, cols] = y.astype(o_ref.dtype)
        return st

    st_ref[...] = lax.fori_loop(0, nc // group, chunks, st0_ref[...])


def _gla(qb, kb, lf, vb, rb, g_norm, st0, group):
    b, t, _ = qb.shape
    assert t % (GLA_CHUNK * group) == 0
    qk = pl.BlockSpec((None, t, LANES), lambda i, j: (i, 0, j))
    vr = pl.BlockSpec((None, t, 2 * DV_B), lambda i, j: (i, 0, j))
    st = pl.BlockSpec((None, None, DV_B, LANES), lambda i, j: (i, j, 0, 0))
    return pl.pallas_call(
        functools.partial(_gla_kernel, group=group),
        grid=(b, H_B // 2),
        in_specs=[qk, qk, qk, vr, vr, pl.BlockSpec((1, 2 * DV_B), lambda i, j: (0, j)), st],
        out_specs=[vr, st],
        out_shape=[jax.ShapeDtypeStruct((b, t, D_VB), BF16), jax.ShapeDtypeStruct(st0.shape, F32)],
        scratch_shapes=[pltpu.VMEM((t, LANES), BF16)] * 4 + [pltpu.VMEM((t // GLA_CHUNK, 8, LANES), F32)],
        compiler_params=_params("parallel", "parallel"),
        name="gla",
    )(qb, kb, lf, vb, rb, g_norm, st0)


def _state_to_pairs(s):
    b = s.shape[0]
    return s.reshape(b, H_B // 2, 2, DK_B, DV_B).transpose(0, 1, 4, 2, 3).reshape(b, H_B // 2, DV_B, 2 * DK_B)


def _pairs_to_state(sp):
    b = sp.shape[0]
    return sp.reshape(b, H_B // 2, DV_B, 2, DK_B).transpose(0, 1, 3, 4, 2).reshape(b, H_B, DK_B, DV_B)


def _gelu(x):
    return 0.5 * x * (1.0 + lax.erf(x * (2.0 ** -0.5)))


def _mix_ffn(oa_ref, ob_ref, x_ref, wo_ref, gains_ref, wg_ref, wu_ref, wd_ref, cw_ref, cb_ref, act_ref,
             fc, shifted, emit_g):
    mixed = _dot(oa_ref[...], wo_ref[0:D_A, :]) + _dot(ob_ref[...], wo_ref[D_A:, :])
    x1 = x_ref[...] + _rms(mixed, gains_ref[0:1, :])
    h = _rms(x1, gains_ref[1:2, :]).astype(BF16)
    d_ff = wg_ref.shape[1]
    for ci in range(d_ff // fc):
        sl = slice(ci * fc, (ci + 1) * fc)
        g = _dot(h, wg_ref[:, sl])
        u = _dot(h, wu_ref[:, sl])
        g1, g2 = shifted(g, sl)
        conv = cb_ref[:, sl] + cw_ref[0:1, sl] * g2 + cw_ref[1:2, sl] * g1 + cw_ref[2:3, sl] * g
        act_ref[:, sl] = (_gelu(conv) * u).astype(BF16)
        emit_g(g, sl)
    return x1 + _rms(_dot(act_ref[...], wd_ref[...]), gains_ref[2:3, :])


def _ffn_prompt_kernel(oa_ref, ob_ref, x_ref, wo_ref, gains_ref, wg_ref, wu_ref, wd_ref, cw_ref, cb_ref,
                       out_ref, cst_ref, carry_ref, act_ref, *, fc):
    tm = x_ref.shape[0]

    @pl.when(pl.program_id(1) == 0)
    def _():
        carry_ref[...] = jnp.zeros_like(carry_ref)

    rows = lax.broadcasted_iota(jnp.int32, (tm, fc), 0)

    def shifted(g, sl):
        prev = carry_ref[:, sl]
        g1 = jnp.where(rows == 0, prev[1:2], pltpu.roll(g, 1, axis=0))
        g2 = jnp.where(rows == 0, prev[0:1], jnp.where(rows == 1, prev[1:2], pltpu.roll(g, 2, axis=0)))
        return g1, g2

    def emit_g(g, sl):
        tail = g[tm - 2:tm]
        carry_ref[0:2, sl] = tail
        cst_ref[:, sl] = tail

    out_ref[...] = _mix_ffn(oa_ref, ob_ref, x_ref, wo_ref, gains_ref, wg_ref, wu_ref, wd_ref, cw_ref, cb_ref,
                            act_ref, fc, shifted, emit_g)


def _ffn_prompt(oa, ob, x, w_o, gains, wg, wu, wd, conv_w, conv_b, tm, fc):
    b, s, d = x.shape
    d_ff = wg.shape[1]
    tok = lambda w: pl.BlockSpec((None, tm, w), lambda i, j: (i, j, 0))
    return pl.pallas_call(
        functools.partial(_ffn_prompt_kernel, fc=fc),
        grid=(b, s // tm),
        in_specs=[tok(D_A), tok(D_VB), tok(d)]
        + [_const_spec(a.shape) for a in (w_o, gains, wg, wu, wd, conv_w, conv_b)],
        out_specs=[tok(d), pl.BlockSpec((None, 2, d_ff), lambda i, j: (i, 0, 0))],
        out_shape=[jax.ShapeDtypeStruct((b, s, d), F32), jax.ShapeDtypeStruct((b, 2, d_ff), F32)],
        scratch_shapes=[pltpu.VMEM((8, d_ff), F32), pltpu.VMEM((tm, d_ff), BF16)],
        compiler_params=_params("parallel", "arbitrary"),
        name="ffn_prompt",
    )(oa, ob, x, w_o, gains, wg, wu, wd, conv_w, conv_b)


def _ffn_sample_kernel(oa_ref, ob_ref, x_ref, p1_ref, p2_ref, wo_ref, gains_ref, wg_ref, wu_ref, wd_ref,
                       cw_ref, cb_ref, out_ref, g_ref, act_ref, *, fc, seg):
    tm = x_ref.shape[0]
    pos = lax.broadcasted_iota(jnp.int32, (tm, fc), 0) % seg

    def shifted(g, sl):
        g1 = jnp.where(pos >= 1, pltpu.roll(g, 1, axis=0), p1_ref[:, sl])
        g2 = jnp.where(pos >= 2, pltpu.roll(g, 2, axis=0), p2_ref[:, sl])
        return g1, g2

    def emit_g(g, sl):
        g_ref[:, sl] = g

    out_ref[...] = _mix_ffn(oa_ref, ob_ref, x_ref, wo_ref, gains_ref, wg_ref, wu_ref, wd_ref, cw_ref, cb_ref,
                            act_ref, fc, shifted, emit_g)


def _ffn_sample(oa, ob, x, conv_state, w_o, gains, wg, wu, wd, conv_w, conv_b, fc):
    b, t, d = x.shape
    d_ff = wg.shape[1]
    assert t >= 2
    zeros = jnp.zeros((b, t - 2, d_ff), F32)
    p1 = jnp.concatenate([conv_state[:, 1:2], zeros, zeros[:, :1]], axis=1).reshape(b * t, d_ff)
    p2 = jnp.concatenate([conv_state, zeros], axis=1).reshape(b * t, d_ff)
    args = (oa.reshape(b * t, -1), ob.reshape(b * t, -1), x.reshape(b * t, d), p1, p2,
            w_o, gains, wg, wu, wd, conv_w, conv_b)
    x2, g = pl.pallas_call(
        functools.partial(_ffn_sample_kernel, fc=fc, seg=t),
        grid=(1,),
        in_specs=[_const_spec(a.shape) for a in args],
        out_specs=[_const_spec((b * t, d)), _const_spec((b * t, d_ff))],
        out_shape=[jax.ShapeDtypeStruct((b * t, d), F32), jax.ShapeDtypeStruct((b * t, d_ff), F32)],
        scratch_shapes=[pltpu.VMEM((b * t, d_ff), BF16)],
        compiler_params=_params("arbitrary"),
        name="ffn_sample",
    )(*args)
    return x2.reshape(b, t, d), g.reshape(b, t, d_ff)


def _pick_tile(n, want):
    tm = min(n, want)
    assert n % tm == 0
    return tm


def _layer(x, weights, tm_want, feature_major_kv):
    b, t, d = x.shape
    x2 = x.reshape(b * t, d)
    tm = _pick_tile(t if feature_major_kv else b * t, tm_want)
    outs = _inproj(x2, weights["g_pre_mix"], weights["w_main"], weights["w_lr"], weights["w_a2"], weights["b_a"], tm,
                   seq=t if feature_major_kv else None)
    proj = [o.reshape(b, t, -1) for o in outs[:8]]
    if feature_major_kv:
        win = [o.reshape(b, H_A, HD_A, t).transpose(0, 3, 1, 2)[None] for o in outs[8:]]
    else:
        win = [proj[1].reshape(1, b, t, H_A, HD_A), proj[2].reshape(1, b, t, H_A, HD_A)]
    return proj + win


def kernel(x_prompt, x_sample, cache_k_win, cache_v_win, state_gla, state_ffn_conv,
           w_in, w_a2, b_a, g_gla_norm, w_o, g_pre_mix, g_post_mix, g_pre_ffn, g_post_ffn,
           w_up, conv_w, conv_b, w_down):
    depth = w_in.shape[0]
    assert depth == 1
    l = 0
    d = x_prompt.shape[-1]
    d_ff = w_down.shape[1]
    fc = 256
    assert d_ff % fc == 0

    w_in_b = w_in[l].astype(BF16)
    wts = {
        "w_main": w_in_b[:, :D_MAIN],
        "w_lr": jnp.pad(w_in_b[:, D_MAIN:], ((0, 0), (0, LANES - GATE_RANK))),
        "w_a2": jnp.pad(w_a2[l].astype(BF16), ((0, LANES - GATE_RANK), (0, 0))),
        "b_a": b_a[l].reshape(1, -1),
        "g_pre_mix": g_pre_mix[l].reshape(1, -1),
    }
    g_norm = g_gla_norm[l].reshape(1, -1)
    w_o_b = w_o[l].astype(BF16)
    gains = jnp.stack([g_post_mix[l], g_pre_ffn[l], g_post_ffn[l]])
    w_up_b = w_up[l].astype(BF16)
    wg, wu = w_up_b[:, :d_ff], w_up_b[:, d_ff:]
    wd = w_down[l].astype(BF16)
    cw = conv_w[l]
    cb = conv_b[l].reshape(1, -1)

    bp, sp, _ = x_prompt.shape
    qa, ka, va, qb, kb, vb, rb, lf, k_win, v_win = _layer(x_prompt, wts, 512, feature_major_kv=True)
    oa = _attn_prompt(qa, ka, va)
    st0 = jnp.zeros((bp, H_B // 2, DV_B, 2 * DK_B), F32)
    ob, st = _gla(qb, kb, lf, vb, rb, g_norm, st0, group=16)
    y_prompt, conv_p = _ffn_prompt(oa, ob, x_prompt, w_o_b, gains, wg, wu, wd, cw, cb, _pick_tile(sp, 512), fc)
    buf = min(max(w for w, _ in PATTERNS), sp)
    win_k_p = k_win[:, :, sp - buf:]
    win_v_p = v_win[:, :, sp - buf:]
    gla_p = _pairs_to_state(st)[None]
    conv_p = conv_p[None]

    bs, ts, _ = x_sample.shape
    qa, ka, va, qb, kb, vb, rb, lf, win_k_s, win_v_s = _layer(x_sample, wts, 256, feature_major_kv=False)
    lbuf = cache_k_win.shape[2]
    feature_major = lambda c: c.transpose(0, 2, 3, 1).reshape(bs, D_A, lbuf)
    oa = _attn_sample(qa, ka, va, feature_major(cache_k_win[l]), feature_major(cache_v_win[l]))
    tpad = -(-ts // GLA_CHUNK) * GLA_CHUNK
    padt = lambda a: jnp.pad(a, ((0, 0), (0, tpad - ts), (0, 0)))
    ob, st = _gla(padt(qb), padt(kb), padt(lf), padt(vb), padt(rb), g_norm, _state_to_pairs(state_gla[l]),
                  group=1)
    ob = ob[:, :ts]
    y_sample, g_s = _ffn_sample(oa, ob, x_sample, state_ffn_conv[l], w_o_b, gains, wg, wu, wd, cw, cb, fc)
    gla_s = _pairs_to_state(st)[None]
    conv_s = g_s[:, ts - 2:][None]

    return (y_prompt, y_sample, win_k_p, win_v_p, gla_p, conv_p, win_k_s, win_v_s, gla_s, conv_s)
```

```python
import functools

import jax
import jax.numpy as jnp
from jax import lax
from jax.experimental import pallas as pl
from jax.experimental.pallas import tpu as pltpu

F32 = jnp.float32
BF16 = jnp.bfloat16

EPS = 1e-6
NEG = -1e30
LOG2_E = 1.4426950408889634
HD_A = 64
H_A = 8
H_B = 4
DK_B = 64
DV_B = 128
GATE_RANK = 16
GLA_TAU = 16.0
PATTERNS = ((128, 1), (512, 4), (2048, 16))
BAND = 128
GLA_CHUNK = 32
LANES = 128
VMEM_LIMIT = 56 * 1024 * 1024

D_A = H_A * HD_A
D_QKB = H_B * DK_B
D_VB = H_B * DV_B
D_MAIN = 3 * D_A + 2 * D_QKB + 2 * D_VB


def _dot(a, b):
    return jnp.dot(a, b, preferred_element_type=F32)


def _dot_nt(a, b):
    return lax.dot_general(a, b, (((1,), (1,)), ((), ())), preferred_element_type=F32)


def _dot_tn(a, b):
    return lax.dot_general(a, b, (((0,), (0,)), ((), ())), preferred_element_type=F32)


def _rms(x, g):
    return x * lax.rsqrt(jnp.mean(x * x, axis=-1, keepdims=True) + EPS) * g


def _params(*sem):
    return pltpu.CompilerParams(dimension_semantics=sem, vmem_limit_bytes=VMEM_LIMIT)


def _const_spec(shape):
    nd = len(shape)
    return pl.BlockSpec(shape, lambda *_: (0,) * nd, pipeline_mode=pl.Buffered(1))


def _inproj_kernel(x_ref, g_ref, w_ref, wlr_ref, wa2_ref, ba_ref,
                   qa_ref, ka_ref, va_ref, qb_ref, kb_ref, vb_ref, rb_ref, lf_ref, *kv_t_refs):
    xn = _rms(x_ref[...], g_ref[...]).astype(BF16)
    off = 0
    for ref in (qa_ref, ka_ref, va_ref, qb_ref, kb_ref, vb_ref, rb_ref):
        n = ref.shape[-1]
        y = _dot(xn, w_ref[:, off:off + n])
        ref[...] = y * (HD_A ** -0.5 * LOG2_E) if ref is qa_ref else y
        off += n
    if kv_t_refs:
        kt_ref, vt_ref = kv_t_refs
        kt_ref[...] = ka_ref[...].T
        vt_ref[...] = va_ref[...].T
    a_lr = _dot(xn, wlr_ref[...])
    z = _dot(a_lr.astype(BF16), wa2_ref[...]) + ba_ref[...]
    lf_ref[...] = (jnp.minimum(z, 0.0) - jnp.log1p(jnp.exp(-jnp.abs(z)))) * (1.0 / GLA_TAU)


def _inproj(x2, g_pre, w_main, w_lr, w_a2, b_a, tm, seq=None):
    n, d = x2.shape
    widths = (D_A, D_A, D_A, D_QKB, D_QKB, D_VB, D_VB, D_QKB)
    row = lambda w: pl.BlockSpec((tm, w), lambda i: (i, 0))
    out_specs = [row(w) for w in widths]
    out_shape = [jax.ShapeDtypeStruct((n, w), F32) for w in widths]
    if seq is not None:
        per = seq // tm
        out_specs += [pl.BlockSpec((None, D_A, tm), lambda i: (i // per, 0, i % per))] * 2
        out_shape += [jax.ShapeDtypeStruct((n // seq, D_A, seq), F32)] * 2
    return pl.pallas_call(
        _inproj_kernel,
        grid=(n // tm,),
        in_specs=[row(d), _const_spec(g_pre.shape), _const_spec(w_main.shape), _const_spec(w_lr.shape),
                  _const_spec(w_a2.shape), _const_spec(b_a.shape)],
        out_specs=out_specs,
        out_shape=out_shape,
        compiler_params=_params("parallel"),
        name="inproj",
    )(x2, g_pre, w_main, w_lr, w_a2, b_a)


class _AttnUnit:
    def __init__(self, q_ref, k_ref, v_ref, dil, start, prev):
        n = BAND
        self.prev = prev

        def rows(s):
            if dil == 1:
                return pl.ds(pl.multiple_of(s, n), n)
            return pl.ds(s, n, stride=dil)

        self.rows = rows(start)
        is_a = lax.broadcasted_iota(jnp.int32, (n, LANES), 1) < HD_A
        q = q_ref[self.rows, :]
        q2 = jnp.concatenate([jnp.where(is_a, q, 0.0), jnp.where(is_a, 0.0, q)], axis=0).astype(BF16)
        k = [k_ref[self.rows, :].astype(BF16)]
        v = [v_ref[self.rows, :].astype(BF16)]
        if prev is not None:
            rows_prev = rows(start - n * dil if prev is True else jnp.where(prev, start - n * dil, start))
            k.append(k_ref[rows_prev, :].astype(BF16))
            v.append(v_ref[rows_prev, :].astype(BF16))
        self.scores = _dot_nt(q2, jnp.concatenate(k, axis=0))
        ones_a = is_a.astype(BF16)
        zero = jnp.zeros((), BF16)
        self.values = jnp.concatenate(
            [jnp.concatenate([jnp.where(is_a, vb, zero), ones_a], axis=1) for vb in v]
            + [jnp.concatenate([jnp.where(is_a, zero, vb), 1 - ones_a], axis=1) for vb in v], axis=0)

    def softmax(self):
        n = BAND
        row = lax.broadcasted_iota(jnp.int32, (2 * n, n), 0) & (n - 1)
        col = lax.broadcasted_iota(jnp.int32, (2 * n, n), 1)
        s = self.scores
        s_own = jnp.where(col <= row, s[:, :n], NEG)
        m = s_own
        if self.prev is not None:
            reach = row if self.prev is True else row + jnp.where(self.prev, 0, n)
            s_prev = jnp.where(col >= reach, s[:, n:], NEG)
            m = jnp.maximum(s_own, s_prev)
        m = jnp.max(m, axis=-1, keepdims=True)
        p = [jnp.exp2(s_own - m).astype(BF16)]
        if self.prev is not None:
            p.append(jnp.exp2(s_prev - m).astype(BF16))
        self.probs = jnp.concatenate([pb[:n] for pb in p] + [pb[n:] for pb in p], axis=1)
        is_a = lax.broadcasted_iota(jnp.int32, (n, LANES), 1) < HD_A
        self.m = jnp.where(is_a, m[:n], m[n:])

    def weighted_values(self):
        out = _dot(self.probs, self.values)
        return self.rows, self.m, out[:, LANES:], out[:, :LANES]


def _attn_prompt_kernel(q_ref, k_ref, v_ref, o_ref, m_scr, l_scr, acc_scr, *, group, lag):
    seq = q_ref.shape[0]
    n = BAND
    order = sorted((dil for _, dil in PATTERNS), reverse=True)
    assert order[-1] == 1
    for p, dil in enumerate(order):
        nb = seq // (n * dil)
        assert (dil * nb) % group == 0 and (group % nb == 0 or nb % group == 0)

        def merge(sl, m2, l2, o2, p=p):
            if p == 0:
                m_scr[sl, :] = m2
                l_scr[sl, :] = l2
                acc_scr[sl, :] = o2
                return
            m0 = m_scr[sl, :]
            mn = jnp.maximum(m0, m2)
            a0 = jnp.exp2(m0 - mn)
            a2 = jnp.exp2(m2 - mn)
            l = a0 * l_scr[sl, :] + a2 * l2
            acc = a0 * acc_scr[sl, :] + a2 * o2
            if p == len(order) - 1:
                o_ref[sl, :] = (acc / l).astype(o_ref.dtype)
            else:
                m_scr[sl, :] = mn
                l_scr[sl, :] = l
                acc_scr[sl, :] = acc

        def units(i, carry, dil=dil, nb=nb, merge=merge):
            where = []
            for j in range(group):
                if group % nb == 0:
                    r = i * (group // nb) + j // nb
                    c = j % nb
                    prev = None if nb == 1 else c > 0
                    if prev is False:
                        prev = None
                else:
                    r = (i * group) // nb
                    c = (i * group) % nb + j
                    prev = True if j > 0 else c > 0
                where.append((r + c * (n * dil), prev))
            inflight = {}
            for s in range(group + 2 * lag):
                if s < group:
                    inflight[s] = _AttnUnit(q_ref, k_ref, v_ref, dil, *where[s])
                if 0 <= s - lag < group:
                    inflight[s - lag].softmax()
                if 0 <= s - 2 * lag < group:
                    merge(*inflight.pop(s - 2 * lag).weighted_values())
            return carry

        lax.fori_loop(0, dil * nb // group, units, 0)


def _attn_prompt(qa, ka, va, group=16, lag=1):
    b, s, _ = qa.shape
    assert s % (BAND * max(d for _, d in PATTERNS)) == 0
    blk = pl.BlockSpec((None, s, LANES), lambda i, j: (i, 0, j))
    return pl.pallas_call(
        functools.partial(_attn_prompt_kernel, group=group, lag=lag),
        grid=(b, D_A // LANES),
        in_specs=[blk, blk, blk],
        out_specs=blk,
        out_shape=jax.ShapeDtypeStruct((b, s, D_A), BF16),
        scratch_shapes=[pltpu.VMEM((s, LANES), F32)] * 3,
        compiler_params=_params("parallel", "parallel"),
        name="attn_prompt",
    )(qa, ka, va)


def _pattern_count(dist):
    cnt = jnp.zeros(dist.shape, jnp.int32)
    for _, dil in PATTERNS:
        shift = dil.bit_length() - 1
        hit = jnp.logical_and((dist & (dil - 1)) == 0, (dist >> shift) <= BAND)
        cnt = cnt + hit.astype(jnp.int32)
    return jnp.where(dist >= 0, cnt, 0)


def _attn_sample_kernel(q_ref, kn_ref, vn_ref, ck_ref, cv_ref, o_ref):
    t = q_ref.shape[0]
    lb = ck_ref.shape[1]
    is_a = lax.broadcasted_iota(jnp.int32, (t, LANES), 1) < HD_A
    tq_c = lax.broadcasted_iota(jnp.int32, (2 * t, lb), 0) % t
    kp_c = lax.broadcasted_iota(jnp.int32, (2 * t, lb), 1)
    cnt_c = _pattern_count(lb + tq_c - kp_c)
    tq_n = lax.broadcasted_iota(jnp.int32, (2 * t, t), 0) % t
    kp_n = lax.broadcasted_iota(jnp.int32, (2 * t, t), 1)
    cnt_n = _pattern_count(tq_n - kp_n)
    w_c, w_n = cnt_c.astype(F32), cnt_n.astype(F32)

    pairs = [slice(j * LANES, (j + 1) * LANES) for j in range(D_A // LANES)]
    scores = []
    for cols in pairs:
        q = q_ref[:, cols]
        q2 =jnp.concatenate([jnp.where(is_a, q, 0.0), jnp.where(is_a, 0.0, q)], axis=0).astype(BF16)
        scores.append((_dot(q2, ck_ref[cols, :].astype(BF16)),
                       _dot_nt(q2, kn_ref[:, cols].astype(BF16))))
    probs = []
    for s_c, s_n in scores:
        m = jnp.maximum(jnp.max(jnp.where(cnt_c > 0, s_c, NEG), axis=-1, keepdims=True),
                        jnp.max(jnp.where(cnt_n > 0, s_n, NEG), axis=-1, keepdims=True))
        e_c = jnp.where(cnt_c > 0, jnp.exp2(s_c - m), 0.0) * w_c
        e_n = jnp.where(cnt_n > 0, jnp.exp2(s_n - m), 0.0) * w_n
        l = jnp.sum(e_c, axis=-1, keepdims=True) + jnp.sum(e_n, axis=-1, keepdims=True)
        probs.append((e_c.astype(BF16), e_n.astype(BF16), l))
    for cols, (e_c, e_n, l) in zip(pairs, probs):
        o = (_dot_nt(e_c, cv_ref[cols, :].astype(BF16)) + _dot(e_n, vn_ref[:, cols].astype(BF16))) / l
        o_ref[:, cols] = jnp.where(is_a, o[:t], o[t:]).astype(o_ref.dtype)


def _attn_sample(qa, ka, va, ck, cv):
    b, t, _ = qa.shape
    lb = ck.shape[2]
    new = pl.BlockSpec((None, t, D_A), lambda i: (i, 0, 0))
    buf = pl.BlockSpec((None, D_A, lb), lambda i: (i, 0, 0))
    return pl.pallas_call(
        _attn_sample_kernel,
        grid=(b,),
        in_specs=[new, new, new, buf, buf],
        out_specs=new,
        out_shape=jax.ShapeDtypeStruct((b, t, D_A), BF16),
        compiler_params=_params("parallel"),
        name="attn_sample",
    )(qa, ka, va, ck, cv)


def _gla_kernel(q_ref, k_ref, lf_ref, v_ref, r_ref, gn_ref, st0_ref, o_ref, st_ref,
                qe_scr, ke_scr, qs_scr, kd_scr, dec_scr, *, group):
    t = q_ref.shape[0]
    c = GLA_CHUNK
    nc = t // c
    half = c // 2

    b = lf_ref[...]
    in_chunk = lax.broadcasted_iota(jnp.int32, (t, LANES), 0) & (c - 1)
    step = 1
    while step < c:
        b = b + jnp.where(in_chunk >= step, pltpu.roll(b, step, axis=0), 0.0)
        step *= 2
    b3 = b.reshape(nc, c, LANES)
    mid = b3[:, half - 1:half, :]
    last = b3[:, c - 1:c, :]
    q3 = (q_ref[...] * (DK_B ** -0.5)).reshape(nc, c, LANES)
    k3 = k_ref[...].reshape(nc, c, LANES)
    qe_scr[...] = (q3 * jnp.exp(b3 - mid)).reshape(t, LANES).astype(BF16)
    ke_scr[...] = (k3 * jnp.exp(mid - b3)).reshape(t, LANES).astype(BF16)
    qs_scr[...] = (q3 * jnp.exp(b3)).reshape(t, LANES).astype(BF16)
    kd_scr[...] = (k3 * jnp.exp(last - b3)).reshape(t, LANES).astype(BF16)
    dec_scr[...] = jnp.broadcast_to(jnp.exp(last), (nc, 8, LANES))

    causal2 = (lax.broadcasted_iota(jnp.int32, (2 * c, c), 1)
               <= (lax.broadcasted_iota(jnp.int32, (2 * c, c), 0) & (c - 1)))
    is_a = lax.broadcasted_iota(jnp.int32, (c, LANES), 1) < DK_B
    gn = gn_ref[...]

    def both_heads(x):
        return jnp.concatenate([jnp.where(is_a, x, 0), jnp.where(is_a, 0, x)], axis=0)

    def chunks(i, st):
        base = i * group
        rows = [pl.ds(pl.multiple_of((base + j) * c, c), c) for j in range(group)]
        vals, att, upd = [], [], []
        for rs in rows:
            v = v_ref[rs, :]
            vals.append((v[:, :DV_B].astype(BF16), v[:, DV_B:].astype(BF16)))
            att.append(_dot_nt(both_heads(qe_scr[rs, :]), ke_scr[rs, :]))
            upd.append(_dot_tn(jnp.concatenate(vals[-1], axis=0), both_heads(kd_scr[rs, :])))
        intra = []
        for (va, vb), a in zip(vals, att):
            a = jnp.where(causal2, a, 0.0).astype(BF16)
            intra.append(jnp.concatenate([_dot(a[:c], va), _dot(a[c:], vb)], axis=1))
        before = []
        for j in range(group):
            before.append(st.astype(BF16))
            st = dec_scr[base + j][0:1, :] * st + upd[j]
        for j, rs in enumerate(rows):
            inter = _dot_nt(both_heads(qs_scr[rs, :]), before[j])
            o = intra[j] + jnp.concatenate([inter[:c], inter[c:]], axis=1)
            r = r_ref[rs, :]
            for e in range(2):
                cols = slice(e * DV_B, (e + 1) * DV_B)
                y = _rms(o[:, cols], gn[:, cols]) * (r[:, cols] * jax.nn.sigmoid(r[:, cols]))
                o_ref[rs, cols] = y.astype(o_ref.dtype)
        return st

    st_ref[...] = lax.fori_loop(0, nc // group, chunks, st0_ref[...])


def _gla(qb, kb, lf, vb, rb, g_norm, st0, group):
    b, t, _ = qb.shape
    assert t % (GLA_CHUNK * group) == 0
    qk = pl.BlockSpec((None, t, LANES), lambda i, j: (i, 0, j))
    vr = pl.BlockSpec((None, t, 2 * DV_B), lambda i, j: (i, 0, j))
    st = pl.BlockSpec((None, None, DV_B, LANES), lambda i, j: (i, j, 0, 0))
    return pl.pallas_call(
        functools.partial(_gla_kernel, group=group),
        grid=(b, H_B // 2),
        in_specs=[qk, qk, qk, vr, vr, pl.BlockSpec((1, 2 * DV_B), lambda i, j: (0, j)), st],
        out_specs=[vr, st],
        out_shape=[jax.ShapeDtypeStruct((b, t, D_VB), BF16), jax.ShapeDtypeStruct(st0.shape, F32)],
        scratch_shapes=[pltpu.VMEM((t, LANES), BF16)] * 4 + [pltpu.VMEM((t // GLA_CHUNK, 8, LANES), F32)],
        compiler_params=_params("parallel", "parallel"),
        name="gla",
    )(qb, kb, lf, vb, rb, g_norm, st0)


def _state_to_pairs(s):
    b = s.shape[0]
    return s.reshape(b, H_B // 2, 2, DK_B, DV_B).transpose(0, 1, 4, 2, 3).reshape(b, H_B // 2, DV_B, 2 * DK_B)


def _pairs_to_state(sp):
    b = sp.shape[0]
    return sp.reshape(b, H_B // 2, DV_B, 2, DK_B).transpose(0, 1, 3, 4, 2).reshape(b, H_B, DK_B, DV_B)


def _gelu(x):
    return 0.5 * x * (1.0 + lax.erf(x * (2.0 ** -0.5)))


def _mix_ffn(oa_ref, ob_ref, x_ref, wo_ref, gains_ref, wg_ref, wu_ref, wd_ref, cw_ref, cb_ref, act_ref,
             fc, shifted, emit_g):
    mixed = _dot(oa_ref[...], wo_ref[0:D_A, :]) + _dot(ob_ref[...], wo_ref[D_A:, :])
    x1 = x_ref[...] + _rms(mixed, gains_ref[0:1, :])
    h = _rms(x1, gains_ref[1:2, :]).astype(BF16)
    d_ff = wg_ref.shape[1]
    for ci in range(d_ff // fc):
        sl = slice(ci * fc, (ci + 1) * fc)
        g = _dot(h, wg_ref[:, sl])
        u = _dot(h, wu_ref[:, sl])
        g1, g2 = shifted(g, sl)
        conv = cb_ref[:, sl] + cw_ref[0:1, sl] * g2 + cw_ref[1:2, sl] * g1 + cw_ref[2:3, sl] * g
        act_ref[:, sl] = (_gelu(conv) * u).astype(BF16)
        emit_g(g, sl)
    return x1 + _rms(_dot(act_ref[...], wd_ref[...]), gains_ref[2:3, :])


def _ffn_prompt_kernel(oa_ref, ob_ref, x_ref, wo_ref, gains_ref, wg_ref, wu_ref, wd_ref, cw_ref, cb_ref,
                       out_ref, cst_ref, carry_ref, act_ref, *, fc):
    tm = x_ref.shape[0]

    @pl.when(pl.program_id(1) == 0)
    def _():
        carry_ref[...] = jnp.zeros_like(carry_ref)

    rows = lax.broadcasted_iota(jnp.int32, (tm, fc), 0)

    def shifted(g, sl):
        prev = carry_ref[:, sl]
        g1 = jnp.where(rows == 0, prev[1:2], pltpu.roll(g, 1, axis=0))
        g2 = jnp.where(rows == 0, prev[0:1], jnp.where(rows == 1, prev[1:2], pltpu.roll(g, 2, axis=0)))
        return g1, g2

    def emit_g(g, sl):
        tail = g[tm - 2:tm]
        carry_ref[0:2, sl] = tail
        cst_ref[:, sl] = tail

    out_ref[...] = _mix_ffn(oa_ref, ob_ref, x_ref, wo_ref, gains_ref, wg_ref, wu_ref, wd_ref, cw_ref, cb_ref,
                            act_ref, fc, shifted, emit_g)


def _ffn_prompt(oa, ob, x, w_o, gains, wg, wu, wd, conv_w, conv_b, tm, fc):
    b, s, d = x.shape
    d_ff = wg.shape[1]
    tok = lambda w: pl.BlockSpec((None, tm, w), lambda i, j: (i, j, 0))
    return pl.pallas_call(
        functools.partial(_ffn_prompt_kernel, fc=fc),
        grid=(b, s // tm),
        in_specs=[tok(D_A), tok(D_VB), tok(d)]
        + [_const_spec(a.shape) for a in (w_o, gains, wg, wu, wd, conv_w, conv_b)],
        out_specs=[tok(d), pl.BlockSpec((None, 2, d_ff), lambda i, j: (i, 0, 0))],
        out_shape=[jax.ShapeDtypeStruct((b, s, d), F32), jax.ShapeDtypeStruct((b, 2, d_ff), F32)],
        scratch_shapes=[pltpu.VMEM((8, d_ff), F32), pltpu.VMEM((tm, d_ff), BF16)],
        compiler_params=_params("parallel", "arbitrary"),
        name="ffn_prompt",
    )(oa, ob, x, w_o, gains, wg, wu, wd, conv_w, conv_b)


def _ffn_sample_kernel(oa_ref, ob_ref, x_ref, p1_ref, p2_ref, wo_ref, gains_ref, wg_ref, wu_ref, wd_ref,
                       cw_ref, cb_ref, out_ref, g_ref, act_ref, *, fc, seg):
    tm = x_ref.shape[0]
    pos = lax.broadcasted_iota(jnp.int32, (tm, fc), 0) % seg

    def shifted(g, sl):
        g1 = jnp.where(pos >= 1, pltpu.roll(g, 1, axis=0), p1_ref[:, sl])
        g2 = jnp.where(pos >= 2, pltpu.roll(g, 2, axis=0), p2_ref[:, sl])
        return g1, g2

    def emit_g(g, sl):
        g_ref[:, sl] = g

    out_ref[...] = _mix_ffn(oa_ref, ob_ref, x_ref, wo_ref, gains_ref, wg_ref, wu_ref, wd_ref, cw_ref, cb_ref,
                            act_ref, fc, shifted, emit_g)


def _ffn_sample(oa, ob, x, conv_state, w_o, gains, wg, wu, wd, conv_w, conv_b, fc):
    b, t, d = x.shape
    d_ff = wg.shape[1]
    assert t >= 2
    zeros = jnp.zeros((b, t - 2, d_ff), F32)
    p1 = jnp.concatenate([conv_state[:, 1:2], zeros, zeros[:, :1]], axis=1).reshape(b * t, d_ff)
    p2 = jnp.concatenate([conv_state, zeros], axis=1).reshape(b * t, d_ff)
    args = (oa.reshape(b * t, -1), ob.reshape(b * t, -1), x.reshape(b * t, d), p1, p2,
            w_o, gains, wg, wu, wd, conv_w, conv_b)
    x2, g = pl.pallas_call(
        functools.partial(_ffn_sample_kernel, fc=fc, seg=t),
        grid=(1,),
        in_specs=[_const_spec(a.shape) for a in args],
        out_specs=[_const_spec((b * t, d)), _const_spec((b * t, d_ff))],
        out_shape=[jax.ShapeDtypeStruct((b * t, d), F32), jax.ShapeDtypeStruct((b * t, d_ff), F32)],
        scratch_shapes=[pltpu.VMEM((b * t, d_ff), BF16)],
        compiler_params=_params("arbitrary"),
        name="ffn_sample",
    )(*args)
    return x2.reshape(b, t, d), g.reshape(b, t, d_ff)


def _pick_tile(n, want):
    tm = min(n, want)
    assert n % tm == 0
    return tm


def _layer(x, weights, tm_want, feature_major_kv):
    b, t, d = x.shape
    x2 = x.reshape(b * t, d)
    tm = _pick_tile(t if feature_major_kv else b * t, tm_want)
    outs = _inproj(x2, weights["g_pre_mix"], weights["w_main"], weights["w_lr"], weights["w_a2"], weights["b_a"], tm,
                   seq=t if feature_major_kv else None)
    proj = [o.reshape(b, t, -1) for o in outs[:8]]
    if feature_major_kv:
        win = [o.reshape(b, H_A, HD_A, t).transpose(0, 3, 1, 2)[None] for o in outs[8:]]
    else:
        win = [proj[1].reshape(1, b, t, H_A, HD_A), proj[2].reshape(1, b, t, H_A, HD_A)]
    return proj + win


def kernel(x_prompt, x_sample, cache_k_win, cache_v_win, state_gla, state_ffn_conv,
           w_in, w_a2, b_a, g_gla_norm, w_o, g_pre_mix, g_post_mix, g_pre_ffn, g_post_ffn,
           w_up, conv_w, conv_b, w_down):
    depth = w_in.shape[0]
    assert depth == 1
    l = 0
    d = x_prompt.shape[-1]
    d_ff = w_down.shape[1]
    fc = 256
    assert d_ff % fc == 0

    w_in_b = w_in[l].astype(BF16)
    wts = {
        "w_main": w_in_b[:, :D_MAIN],
        "w_lr": jnp.pad(w_in_b[:, D_MAIN:], ((0, 0), (0, LANES - GATE_RANK))),
        "w_a2": jnp.pad(w_a2[l].astype(BF16), ((0, LANES - GATE_RANK), (0, 0))),
        "b_a": b_a[l].reshape(1, -1),
        "g_pre_mix": g_pre_mix[l].reshape(1, -1),
    }
    g_norm = g_gla_norm[l].reshape(1, -1)
    w_o_b = w_o[l].astype(BF16)
    gains = jnp.stack([g_post_mix[l], g_pre_ffn[l], g_post_ffn[l]])
    w_up_b = w_up[l].astype(BF16)
    wg, wu = w_up_b[:, :d_ff], w_up_b[:, d_ff:]
    wd = w_down[l].astype(BF16)
    cw = conv_w[l]
    cb = conv_b[l].reshape(1, -1)

    bp, sp, _ = x_prompt.shape
    qa, ka, va, qb, kb, vb, rb, lf, k_win, v_win = _layer(x_prompt, wts, 512, feature_major_kv=True)
    oa = _attn_prompt(qa, ka, va)
    st0 = jnp.zeros((bp, H_B // 2, DV_B, 2 * DK_B), F32)
    ob, st = _gla(qb, kb, lf, vb, rb, g_norm, st0, group=16)
    y_prompt, conv_p = _ffn_prompt(oa, ob, x_prompt, w_o_b, gains, wg, wu, wd, cw, cb, _pick_tile(sp, 512), fc)
    buf = min(max(w for w, _ in PATTERNS), sp)
    win_k_p = k_win[:, :, sp - buf:]
    win_v_p = v_win[:, :, sp - buf:]
    gla_p = _pairs_to_state(st)[None]
    conv_p = conv_p[None]

    bs, ts, _ = x_sample.shape
    qa, ka, va, qb, kb, vb, rb, lf, win_k_s, win_v_s = _layer(x_sample, wts, 256, feature_major_kv=False)
    lbuf = cache_k_win.shape[2]
    feature_major = lambda c: c.transpose(0, 2, 3, 1).reshape(bs, D_A, lbuf)
    oa = _attn_sample(qa, ka, va, feature_major(cache_k_win[l]), feature_major(cache_v_win[l]))
    tpad = -(-ts // GLA_CHUNK) * GLA_CHUNK
    padt = lambda a: jnp.pad(a, ((0, 0), (0, tpad - ts), (0, 0)))
    ob, st = _gla(padt(qb), padt(kb), padt(lf), padt(vb), padt(rb), g_norm, _state_to_pairs(state_gla[l]),
                  group=1)
    ob = ob[:, :ts]
    y_sample, g_s = _ffn_sample(oa, ob, x_sample, state_ffn_conv[l], w_o_b, gains, wg, wu, wd, cw, cb, fc)
    gla_s = _pairs_to_state(st)[None]
    conv_s = g_s[:, ts - 2:][None]

    return (y_prompt, y_sample, win_k_p, win_v_p, gla_p, conv_p, win_k_s, win_v_s, gla_s, conv_s)
```

```python
import functools
import math

import jax
import jax.numpy as jnp
from jax import lax
from jax.experimental import pallas as pl
from jax.experimental.pallas import tpu as pltpu

F32 = jnp.float32
BF16 = jnp.bfloat16

EPS = 1e-6
NEG = -1e30
LOG2_E = 1.4426950408889634
HD_A = 64
H_A = 8
H_B = 4
DK_B = 64
DV_B = 128
GATE_RANK = 16
GLA_TAU = 16.0
PATTERNS = ((128, 1), (512, 4), (2048, 16))
BAND = 128
GLA_CHUNK = 32
LANES = 128
VMEM_LIMIT = 56 * 1024 * 1024

D_A = H_A * HD_A
D_QKB = H_B * DK_B
D_VB = H_B * DV_B
D_MAIN = 3 * D_A + 2 * D_QKB + 2 * D_VB


def _dot(a, b):
    return jnp.dot(a, b, preferred_element_type=F32)


def _dot_nt(a, b):
    return lax.dot_general(a, b, (((1,), (1,)), ((), ())), preferred_element_type=F32)


def _dot_tn(a, b):
    return lax.dot_general(a, b, (((0,), (0,)), ((), ())), preferred_element_type=F32)


def _rms(x, g):
    return x * lax.rsqrt(jnp.mean(x * x, axis=-1, keepdims=True) + EPS) * g


def _params(*sem):
    return pltpu.CompilerParams(dimension_semantics=sem, vmem_limit_bytes=VMEM_LIMIT)


def _const_spec(shape):
    nd = len(shape)
    return pl.BlockSpec(shape, lambda *_: (0,) * nd, pipeline_mode=pl.Buffered(1))


def _inproj_kernel(x_ref, g_ref, w_ref, wlr_ref, wa2_ref, ba_ref,
                   qa_ref, ka_ref, va_ref, qb_ref, kb_ref, vb_ref, rb_ref, lf_ref, *kv_t_refs):
    xn = _rms(x_ref[...], g_ref[...]).astype(BF16)
    a_lr = _dot(xn, wlr_ref[...])
    off = 0
    for ref in (qa_ref, ka_ref, va_ref, qb_ref, kb_ref, vb_ref, rb_ref):
        n = ref.shape[-1]
        y = _dot(xn, w_ref[:, off:off + n])
        ref[...] = y * (HD_A ** -0.5 * LOG2_E) if ref is qa_ref else y
        off += n
        if ref is qa_ref:
            z = _dot(a_lr.astype(BF16), wa2_ref[...]) + ba_ref[...]
            lf_ref[...] = (jnp.minimum(z, 0.0) - jnp.log1p(jnp.exp(-jnp.abs(z)))) * (1.0 / GLA_TAU)
    if kv_t_refs:
        kt_ref, vt_ref = kv_t_refs
        kt_ref[...] = ka_ref[...].T
        vt_ref[...] = va_ref[...].T


def _inproj(x2, g_pre, w_main, w_lr, w_a2, b_a, tm, seq=None):
    n, d = x2.shape
    widths = (D_A, D_A, D_A, D_QKB, D_QKB, D_VB, D_VB, D_QKB)
    row = lambda w: pl.BlockSpec((tm, w), lambda i: (i, 0))
    out_specs = [row(w) for w in widths]
    out_shape = [jax.ShapeDtypeStruct((n, w), F32) for w in widths]
    if seq is not None:
        per = seq // tm
        out_specs += [pl.BlockSpec((None, D_A, tm), lambda i: (i // per, 0, i % per))] * 2
        out_shape += [jax.ShapeDtypeStruct((n // seq, D_A, seq), F32)] * 2
    return pl.pallas_call(
        _inproj_kernel,
        grid=(n // tm,),
        in_specs=[row(d), _const_spec(g_pre.shape), _const_spec(w_main.shape), _const_spec(w_lr.shape),
                  _const_spec(w_a2.shape), _const_spec(b_a.shape)],
        out_specs=out_specs,
        out_shape=out_shape,
        compiler_params=_params("parallel"),
        name="inproj",
    )(x2, g_pre, w_main, w_lr, w_a2, b_a)


class _AttnUnit:
    def __init__(self, q_ref, k_ref, v_ref, dil, start, prev):
        n = BAND
        self.prev = prev

        def rows(s):
            if dil == 1:
                return pl.ds(pl.multiple_of(s, n), n)
            return pl.ds(s, n, stride=dil)

        self.rows = rows(start)
        is_a = lax.broadcasted_iota(jnp.int32, (n, LANES), 1) < HD_A
        q = q_ref[self.rows, :]
        q2 = jnp.concatenate([jnp.where(is_a, q, 0.0), jnp.where(is_a, 0.0, q)], axis=0).astype(BF16)
        k = [k_ref[self.rows, :].astype(BF16)]
        v = [v_ref[self.rows, :].astype(BF16)]
        if prev is not None:
            rows_prev = rows(start - n * dil if prev is True else jnp.where(prev, start - n * dil, start))
            k.append(k_ref[rows_prev, :].astype(BF16))
            v.append(v_ref[rows_prev, :].astype(BF16))
        self.scores = _dot_nt(q2, jnp.concatenate(k, axis=0))
        ones_a = is_a.astype(BF16)
        zero = jnp.zeros((), BF16)
        self.values = jnp.concatenate(
            [jnp.concatenate([jnp.where(is_a, vb, zero), ones_a], axis=1) for vb in v]
            + [jnp.concatenate([jnp.where(is_a, zero, vb), 1 - ones_a], axis=1) for vb in v], axis=0)

    def softmax(self):
        n = BAND
        row = lax.broadcasted_iota(jnp.int32, (2 * n, n), 0) & (n - 1)
        col = lax.broadcasted_iota(jnp.int32, (2 * n, n), 1)
        s = self.scores
        s_own = jnp.where(col <= row, s[:, :n], NEG)
        m = s_own
        if self.prev is not None:
            reach = row if self.prev is True else row + jnp.where(self.prev, 0, n)
            s_prev = jnp.where(col >= reach, s[:, n:], NEG)
            m = jnp.maximum(s_own, s_prev)
        m = jnp.max(m, axis=-1, keepdims=True)
        p = [jnp.exp2(s_own - m).astype(BF16)]
        if self.prev is not None:
            p.append(jnp.exp2(s_prev - m).astype(BF16))
        self.probs = jnp.concatenate([pb[:n] for pb in p] + [pb[n:] for pb in p], axis=1)
        is_a = lax.broadcasted_iota(jnp.int32, (n, LANES), 1) < HD_A
        self.m = jnp.where(is_a, m[:n], m[n:])

    def weighted_values(self):
        out = _dot(self.probs, self.values)
        return self.rows, self.m, out[:, LANES:], out[:, :LANES]


def _attn_prompt_kernel(q_ref, k_ref, v_ref, o_ref, m_scr, l_scr, acc_scr, *, group, lag):
    seq = q_ref.shape[0]
    n = BAND
    order = sorted((dil for _, dil in PATTERNS), reverse=True)
    assert order[-1] == 1
    for p, dil in enumerate(order):
        nb = seq // (n * dil)
        assert (dil * nb) % group == 0 and (group % nb == 0 or nb % group == 0)

        def merge(sl, m2, l2, o2, p=p):
            if p == 0:
                m_scr[sl, :] = m2
                l_scr[sl, :] = l2
                acc_scr[sl, :] = o2
                return
            m0 = m_scr[sl, :]
            mn = jnp.maximum(m0, m2)
            a0 = jnp.exp2(m0 - mn)
            a2 = jnp.exp2(m2 - mn)
            l = a0 * l_scr[sl, :] + a2 * l2
            acc = a0 * acc_scr[sl, :] + a2 * o2
            if p == len(order) - 1:
                o_ref[sl, :] = (acc / l).astype(o_ref.dtype)
            else:
                m_scr[sl, :] = mn
                l_scr[sl, :] = l
                acc_scr[sl, :] = acc

        def units(i, carry, dil=dil, nb=nb, merge=merge):
            where = []
            for j in range(group):
                if group % nb == 0:
                    r = i * (group // nb) + j // nb
                    c = j % nb
                    prev = None if nb == 1 else c > 0
                    if prev is False:
                        prev = None
                else:
                    r = (i * group) // nb
                    c = (i * group) % nb + j
                    prev = True if j > 0 else c > 0
                where.append((r + c * (n * dil), prev))
            inflight = {}
            for s in range(group + 2 * lag):
                if s < group:
                    inflight[s] = _AttnUnit(q_ref, k_ref, v_ref, dil, *where[s])
                if 0 <= s - lag < group:
                    inflight[s - lag].softmax()
                if 0 <= s - 2 * lag < group:
                    merge(*inflight.pop(s - 2 * lag).weighted_values())
            return carry

        lax.fori_loop(0, dil * nb // group, units, 0)


def _attn_prompt(qa, ka, va, group=16, lag=1):
    b, s, _ = qa.shape
    assert s % (BAND * max(d for _, d in PATTERNS)) == 0
    blk = pl.BlockSpec((None, s, LANES), lambda i, j: (i, 0, j))
    return pl.pallas_call(
        functools.partial(_attn_prompt_kernel, group=group, lag=lag),
        grid=(b, D_A // LANES),
        in_specs=[blk, blk, blk],
        out_specs=blk,
        out_shape=jax.ShapeDtypeStruct((b, s, D_A), BF16),
        scratch_shapes=[pltpu.VMEM((s, LANES), F32)] * 3,
        compiler_params=_params("parallel", "parallel"),
        name="attn_prompt",
    )(qa, ka, va)


def _pattern_count(dist):
    cnt = jnp.zeros(dist.shape, jnp.int32)
    for _, dil in PATTERNS:
        shift = dil.bit_length() - 1
        hit = jnp.logical_and((dist & (dil - 1)) == 0, (dist >> shift) <= BAND)
        cnt = cnt + hit.astype(jnp.int32)
    return jnp.where(dist >= 0, cnt, 0)


def _attn_sample_kernel(q_ref, kn_ref, vn_ref, ck_ref, cv_ref, o_ref):
    t = q_ref.shape[0]
    lb = ck_ref.shape[1]
    is_a = lax.broadcasted_iota(jnp.int32, (t, LANES), 1) < HD_A
    tq_c = lax.broadcasted_iota(jnp.int32, (2 * t, lb), 0) % t
    kp_c = lax.broadcasted_iota(jnp.int32, (2 * t, lb), 1)
    cnt_c = _pattern_count(lb + tq_c - kp_c)
    tq_n = lax.broadcasted_iota(jnp.int32, (2 * t, t), 0) % t
    kp_n = lax.broadcasted_iota(jnp.int32, (2 * t, t), 1)
    cnt_n = _pattern_count(tq_n - kp_n)
    w_c, w_n = cnt_c.astype(F32), cnt_n.astype(F32)

    pairs = [slice(j * LANES, (j + 1) * LANES) for j in range(D_A // LANES)]
    scores = []
    for cols in pairs:
        q = q_ref[:, cols]
        q2 =jnp.concatenate([jnp.where(is_a, q, 0.0), jnp.where(is_a, 0.0, q)], axis=0).astype(BF16)
        scores.append((_dot(q2, ck_ref[cols, :].astype(BF16)),
                       _dot_nt(q2, kn_ref[:, cols].astype(BF16))))
    probs = []
    for s_c, s_n in scores:
        m = jnp.maximum(jnp.max(jnp.where(cnt_c > 0, s_c, NEG), axis=-1, keepdims=True),
                        jnp.max(jnp.where(cnt_n > 0, s_n, NEG), axis=-1, keepdims=True))
        e_c = jnp.where(cnt_c > 0, jnp.exp2(s_c - m), 0.0) * w_c
        e_n = jnp.where(cnt_n > 0, jnp.exp2(s_n - m), 0.0) * w_n
        l = jnp.sum(e_c, axis=-1, keepdims=True) + jnp.sum(e_n, axis=-1, keepdims=True)
        probs.append((e_c.astype(BF16), e_n.astype(BF16), l))
    for cols, (e_c, e_n, l) in zip(pairs, probs):
        o = (_dot_nt(e_c, cv_ref[cols, :].astype(BF16)) + _dot(e_n, vn_ref[:, cols].astype(BF16))) / l
        o_ref[:, cols] = jnp.where(is_a, o[:t], o[t:]).astype(o_ref.dtype)


def _attn_sample(qa, ka, va, ck, cv):
    b, t, _ = qa.shape
    lb = ck.shape[2]
    new = pl.BlockSpec((None, t, D_A), lambda i: (i, 0, 0))
    buf = pl.BlockSpec((None, D_A, lb), lambda i: (i, 0, 0))
    return pl.pallas_call(
        _attn_sample_kernel,
        grid=(b,),
        in_specs=[new, new, new, buf, buf],
        out_specs=new,
        out_shape=jax.ShapeDtypeStruct((b, t, D_A), BF16),
        compiler_params=_params("parallel"),
        name="attn_sample",
    )(qa, ka, va, ck, cv)


def _gla_kernel(q_ref, k_ref, lf_ref, v_ref, r_ref, gn_ref, st0_ref, o_ref, st_ref,
                qe_scr, ke_scr, qs_scr, kd_scr, dec_scr, *, group, scan):
    t = q_ref.shape[0]
    c = GLA_CHUNK
    nc = t // c
    half = c // 2

    b = lf_ref[...]
    in_chunk = lax.broadcasted_iota(jnp.int32, (t, LANES), 0) & (c - 1)
    step = 1
    while step < c:
        b = b + jnp.where(in_chunk >= step, pltpu.roll(b, step, axis=0), 0.0)
        step *= 2
    b3 = b.reshape(nc, c, LANES)
    mid = b3[:, half - 1:half, :]
    last = b3[:, c - 1:c, :]
    q3 = (q_ref[...] * (DK_B ** -0.5)).reshape(nc, c, LANES)
    k3 = k_ref[...].reshape(nc, c, LANES)
    qe_scr[...] = (q3 * jnp.exp(b3 - mid)).reshape(t, LANES).astype(BF16)
    ke_scr[...] = (k3 * jnp.exp(mid - b3)).reshape(t, LANES).astype(BF16)
    qs_scr[...] = (q3 * jnp.exp(b3)).reshape(t, LANES).astype(BF16)
    kd_scr[...] = (k3 * jnp.exp(last - b3)).reshape(t, LANES).astype(BF16)
    dec_scr[...] = jnp.broadcast_to(jnp.exp(last), (nc, 8, LANES))

    causal2 = (lax.broadcasted_iota(jnp.int32, (2 * c, c), 1)
               <= (lax.broadcasted_iota(jnp.int32, (2 * c, c), 0) & (c - 1)))
    is_a = lax.broadcasted_iota(jnp.int32, (c, LANES), 1) < DK_B
    gn = gn_ref[...]

    def both_heads(x):
        return jnp.concatenate([jnp.where(is_a, x, 0), jnp.where(is_a, 0, x)], axis=0)

    def chunks(i, st):
        base = i * group
        rows = [pl.ds(pl.multiple_of((base + j) * c, c), c) for j in range(group)]
        vals, att, upd = [], [], []
        for rs in rows:
            v = v_ref[rs, :]
            vals.append((v[:, :DV_B].astype(BF16), v[:, DV_B:].astype(BF16)))
            att.append(_dot_nt(both_heads(qe_scr[rs, :]), ke_scr[rs, :]))
            upd.append(_dot_tn(jnp.concatenate(vals[-1], axis=0), both_heads(kd_scr[rs, :])))
        intra = []
        for (va, vb), a in zip(vals, att):
            a = jnp.where(causal2, a, 0.0).astype(BF16)
            intra.append(jnp.concatenate([_dot(a[:c], va), _dot(a[c:], vb)], axis=1))
        before = []
        for j in range(group):
            if not scan:
                st = st0_ref[base + j]
            before.append(st.astype(BF16))
            st = dec_scr[base + j][0:1, :] * st + upd[j]
            if not scan:
                st_ref[base + j] = st
        for j, rs in enumerate(rows):
            inter = _dot_nt(both_heads(qs_scr[rs, :]), before[j])
            o = intra[j] + jnp.concatenate([inter[:c], inter[c:]], axis=1)
            r = r_ref[rs, :]
            for e in range(2):
                cols = slice(e * DV_B, (e + 1) * DV_B)
                y = _rms(o[:, cols], gn[:, cols]) * (r[:, cols] * jax.nn.sigmoid(r[:, cols]))
                o_ref[rs, cols] = y.astype(o_ref.dtype)
        return st

    if scan:
        st_ref[...] = lax.fori_loop(0, nc // group, chunks, st0_ref[...])
    else:
        lax.fori_loop(0, nc // group, chunks, jnp.zeros((DV_B, LANES), F32))


def _gla(qb, kb, lf, vb, rb, g_norm, st0, group, scan):
    b, t, _ = qb.shape
    assert t % (GLA_CHUNK * group) == 0
    qk = pl.BlockSpec((None, t, LANES), lambda i, j: (i, 0, j))
    vr = pl.BlockSpec((None, t, 2 * DV_B), lambda i, j: (i, 0, j))
    if scan:
        st = pl.BlockSpec((None, None, DV_B, LANES), lambda i, j: (i, j, 0, 0))
    else:
        assert b == 1 and st0.shape[0] == t // GLA_CHUNK
        st = pl.BlockSpec((t // GLA_CHUNK, None, DV_B, LANES), lambda i, j: (0, j, 0, 0))
    return pl.pallas_call(
        functools.partial(_gla_kernel, group=group, scan=scan),
        grid=(b, H_B // 2),
        in_specs=[qk, qk, qk, vr, vr, pl.BlockSpec((1, 2 * DV_B), lambda i, j: (0, j)), st],
        out_specs=[vr, st],
        out_shape=[jax.ShapeDtypeStruct((b, t, D_VB), BF16), jax.ShapeDtypeStruct(st0.shape, F32)],
        scratch_shapes=[pltpu.VMEM((t, LANES), BF16)] * 4 + [pltpu.VMEM((t // GLA_CHUNK, 8, LANES), F32)],
        compiler_params=_params("parallel", "parallel"),
        name="gla",
    )(qb, kb, lf, vb, rb, g_norm, st0)


def _state_to_pairs(s):
    b = s.shape[0]
    return s.reshape(b, H_B // 2, 2, DK_B, DV_B).transpose(0, 1, 4, 2, 3).reshape(b, H_B // 2, DV_B, 2 * DK_B)


def _pairs_to_state(sp):
    b = sp.shape[0]
    return sp.reshape(b, H_B // 2, DV_B, 2, DK_B).transpose(0, 1, 3, 4, 2).reshape(b, H_B, DK_B, DV_B)


def _gelu(x):
    return 0.5 * x * (1.0 + lax.erf(x * (2.0 ** -0.5)))


def _mix_ffn(oa_ref, ob_ref, x_ref, wo_ref, gains_ref, wg_ref, wu_ref, wd_ref, cw_ref, cb_ref, act_ref, out_ref,
             fc, parts, shifted, emit_g):
    heads = []
    for rows in parts:
        mixed = _dot(oa_ref[rows, :], wo_ref[0:D_A, :]) + _dot(ob_ref[rows, :], wo_ref[D_A:, :])
        x1 = x_ref[rows, :] + _rms(mixed, gains_ref[0:1, :])
        heads.append((x1, _rms(x1, gains_ref[1:2, :]).astype(BF16)))
    d_ff = wg_ref.shape[1]
    for rows, (x1, h) in zip(parts, heads):
        for ci in range(d_ff // fc):
            sl = slice(ci * fc, (ci + 1) * fc)
            g = _dot(h, wg_ref[:, sl])
            u = _dot(h, wu_ref[:, sl])
            g1, g2 = shifted(g, sl, rows)
            conv = cb_ref[:, sl] + cw_ref[0:1, sl] * g2 + cw_ref[1:2, sl] * g1 + cw_ref[2:3, sl] * g
            act_ref[rows, sl] = (_gelu(conv) * u).astype(BF16)
            emit_g(g, sl, rows)
        out_ref[rows, :] = x1 + _rms(_dot(act_ref[rows, :], wd_ref[...]), gains_ref[2:3, :])


def _ffn_prompt_kernel(oa_ref, ob_ref, x_ref, wo_ref, gains_ref, wg_ref, wu_ref, wd_ref, cw_ref, cb_ref,
                       out_ref, cst_ref, carry_ref, act_ref, *, fc, n_parts):
    tm = x_ref.shape[0]
    part = tm // n_parts

    @pl.when(pl.program_id(1) == 0)
    def _():
        carry_ref[...] = jnp.zeros_like(carry_ref)

    row = lax.broadcasted_iota(jnp.int32, (part, fc), 0)

    def shifted(g, sl, rows):
        prev = carry_ref[:, sl]
        g1 = jnp.where(row == 0, prev[1:2], pltpu.roll(g, 1, axis=0))
        g2 = jnp.where(row == 0, prev[0:1], jnp.where(row == 1, prev[1:2], pltpu.roll(g, 2, axis=0)))
        return g1, g2

    def emit_g(g, sl, rows):
        tail = g[part - 2:part]
        carry_ref[0:2, sl] = tail
        cst_ref[:, sl] = tail

    _mix_ffn(oa_ref, ob_ref, x_ref, wo_ref, gains_ref, wg_ref, wu_ref, wd_ref, cw_ref, cb_ref, act_ref, out_ref,
             fc, [slice(i * part, (i + 1) * part) for i in range(n_parts)], shifted, emit_g)


def _ffn_prompt(oa, ob, x, w_o, gains, wg, wu, wd, conv_w, conv_b, tm, fc, n_parts):
    b, s, d = x.shape
    d_ff = wg.shape[1]
    tok = lambda w: pl.BlockSpec((None, tm, w), lambda i, j: (i, j, 0))
    return pl.pallas_call(
        functools.partial(_ffn_prompt_kernel, fc=fc, n_parts=n_parts),
        grid=(b, s // tm),
        in_specs=[tok(D_A), tok(D_VB), tok(d)]
        + [_const_spec(a.shape) for a in (w_o, gains, wg, wu, wd, conv_w, conv_b)],
        out_specs=[tok(d), pl.BlockSpec((None, 2, d_ff), lambda i, j: (i, 0, 0))],
        out_shape=[jax.ShapeDtypeStruct((b, s, d), F32), jax.ShapeDtypeStruct((b, 2, d_ff), F32)],
        scratch_shapes=[pltpu.VMEM((8, d_ff), F32), pltpu.VMEM((tm, d_ff), BF16)],
        compiler_params=_params("parallel", "arbitrary"),
        name="ffn_prompt",
    )(oa, ob, x, w_o, gains, wg, wu, wd, conv_w, conv_b)


def _ffn_sample_kernel(oa_ref, ob_ref, x_ref, p1_ref, p2_ref, wo_ref, gains_ref, wg_ref, wu_ref, wd_ref,
                       cw_ref, cb_ref, out_ref, g_ref, act_ref, *, fc, seg):
    tm = x_ref.shape[0]
    pos = lax.broadcasted_iota(jnp.int32, (tm, fc), 0) % seg

    def shifted(g, sl, rows):
        g1 = jnp.where(pos >= 1, pltpu.roll(g, 1, axis=0), p1_ref[rows, sl])
        g2 = jnp.where(pos >= 2, pltpu.roll(g, 2, axis=0), p2_ref[rows, sl])
        return g1, g2

    def emit_g(g, sl, rows):
        g_ref[rows, sl] = g

    _mix_ffn(oa_ref, ob_ref, x_ref, wo_ref, gains_ref, wg_ref, wu_ref, wd_ref, cw_ref, cb_ref, act_ref, out_ref,
             fc, [slice(0, tm)], shifted, emit_g)


def _ffn_sample(oa, ob, x, conv_state, w_o, gains, wg, wu, wd, conv_w, conv_b, fc):
    b, t, d = x.shape
    d_ff = wg.shape[1]
    assert t >= 2
    zeros = jnp.zeros((b, t - 2, d_ff), F32)
    p1 = jnp.concatenate([conv_state[:, 1:2], zeros, zeros[:, :1]], axis=1).reshape(b * t, d_ff)
    p2 = jnp.concatenate([conv_state, zeros], axis=1).reshape(b * t, d_ff)
    args = (oa.reshape(b * t, -1), ob.reshape(b * t, -1), x.reshape(b * t, d), p1, p2,
            w_o, gains, wg, wu, wd, conv_w, conv_b)
    x2, g = pl.pallas_call(
        functools.partial(_ffn_sample_kernel, fc=fc, seg=t),
        grid=(1,),
        in_specs=[_const_spec(a.shape) for a in args],
        out_specs=[_const_spec((b * t, d)), _const_spec((b * t, d_ff))],
        out_shape=[jax.ShapeDtypeStruct((b * t, d), F32), jax.ShapeDtypeStruct((b * t, d_ff), F32)],
        scratch_shapes=[pltpu.VMEM((b * t, d_ff), BF16)],
        compiler_params=_params("arbitrary"),
        name="ffn_sample",
    )(*args)
    return x2.reshape(b, t, d), g.reshape(b, t, d_ff)


def _pick_tile(n, want):
    tm = min(n, want)
    assert n % tm == 0
    return tm


def _layer(x, weights, tm_want, feature_major_kv):
    b, t, d = x.shape
    x2 = x.reshape(b * t, d)
    tm = _pick_tile(t if feature_major_kv else b * t, tm_want)
    outs = _inproj(x2, weights["g_pre_mix"], weights["w_main"], weights["w_lr"], weights["w_a2"], weights["b_a"], tm,
                   seq=t if feature_major_kv else None)
    proj = [o.reshape(b, t, -1) for o in outs[:8]]
    if feature_major_kv:
        win = [o.reshape(b, H_A, HD_A, t).transpose(0, 3, 1, 2)[None] for o in outs[8:]]
    else:
        win = [proj[1].reshape(1, b, t, H_A, HD_A), proj[2].reshape(1, b, t, H_A, HD_A)]
    return proj + win


def kernel(x_prompt, x_sample, cache_k_win, cache_v_win, state_gla, state_ffn_conv,
           w_in, w_a2, b_a, g_gla_norm, w_o, g_pre_mix, g_post_mix, g_pre_ffn, g_post_ffn,
           w_up, conv_w, conv_b, w_down):
    depth = w_in.shape[0]
    assert depth == 1
    l = 0
    d = x_prompt.shape[-1]
    d_ff = w_down.shape[1]
    fc = 256
    assert d_ff % fc == 0

    w_in_b = w_in[l].astype(BF16)
    wts = {
        "w_main": w_in_b[:, :D_MAIN],
        "w_lr": jnp.pad(w_in_b[:, D_MAIN:], ((0, 0), (0, LANES - GATE_RANK))),
        "w_a2": jnp.pad(w_a2[l].astype(BF16), ((0, LANES - GATE_RANK), (0, 0))),
        "b_a": b_a[l].reshape(1, -1),
        "g_pre_mix": g_pre_mix[l].reshape(1, -1),
    }
    g_norm = g_gla_norm[l].reshape(1, -1)
    w_o_b = w_o[l].astype(BF16)
    gains = jnp.stack([g_post_mix[l], g_pre_ffn[l], g_post_ffn[l]])
    w_up_b = w_up[l].astype(BF16)
    wg, wu = w_up_b[:, :d_ff], w_up_b[:, d_ff:]
    wd = w_down[l].astype(BF16)
    cw = conv_w[l]
    cb = conv_b[l].reshape(1, -1)

    bp, sp, _ = x_prompt.shape
    qa, ka, va, qb, kb, vb, rb, lf, k_win, v_win = _layer(x_prompt, wts, 512, feature_major_kv=True)
    oa = _attn_prompt(qa, ka, va)
    st0 = jnp.zeros((bp, H_B // 2, DV_B, 2 * DK_B), F32)
    ob, st = _gla(qb, kb, lf, vb, rb, g_norm, st0, group=sp // GLA_CHUNK, scan=True)
    y_prompt, conv_p = _ffn_prompt(oa, ob, x_prompt, w_o_b, gains, wg, wu, wd, cw, cb, _pick_tile(sp, 1024), fc,
                                   n_parts=2)
    buf = min(max(w for w, _ in PATTERNS), sp)
    win_k_p = k_win[:, :, sp - buf:]
    win_v_p = v_win[:, :, sp - buf:]
    gla_p = _pairs_to_state(st)[None]
    conv_p = conv_p[None]

    bs, ts, _ = x_sample.shape
    qa, ka, va, qb, kb, vb, rb, lf, win_k_s, win_v_s = _layer(x_sample, wts, 256, feature_major_kv=False)
    lbuf = cache_k_win.shape[2]
    feature_major = lambda c: c.transpose(0, 2, 3, 1).reshape(bs, D_A, lbuf)
    oa = _attn_sample(qa, ka, va, feature_major(cache_k_win[l]), feature_major(cache_v_win[l]))
    assert ts <= GLA_CHUNK
    padt = lambda a: jnp.pad(a, ((0, 0), (0, GLA_CHUNK - ts), (0, 0))).reshape(1, bs * GLA_CHUNK, -1)
    ob, st = _gla(padt(qb), padt(kb), padt(lf), padt(vb), padt(rb), g_norm, _state_to_pairs(state_gla[l]),
                  group=math.gcd(bs, 16), scan=False)
    ob = ob.reshape(bs, GLA_CHUNK, -1)[:, :ts]
    y_sample, g_s = _ffn_sample(oa, ob, x_sample, state_ffn_conv[l], w_o_b, gains, wg, wu, wd, cw, cb, fc)
    gla_s = _pairs_to_state(st)[None]
    conv_s = g_s[:, ts - 2:][None]

    return (y_prompt, y_sample, win_k_p, win_v_p, gla_p, conv_p, win_k_s, win_v_s, gla_s, conv_s)
```

```python
import functools
import math

import jax
import jax.numpy as jnp
from jax import lax
from jax.experimental import pallas as pl
from jax.experimental.pallas import tpu as pltpu

F32 = jnp.float32
BF16 = jnp.bfloat16

EPS = 1e-6
NEG = -1e30
LOG2_E = 1.4426950408889634
HD_A = 64
H_A = 8
H_B = 4
DK_B = 64
DV_B = 128
GATE_RANK = 16
GLA_TAU = 16.0
PATTERNS = ((128, 1), (512, 4), (2048, 16))
BAND = 128
GLA_CHUNK = 32
LANES = 128
VMEM_LIMIT = 56 * 1024 * 1024

D_A = H_A * HD_A
D_QKB = H_B * DK_B
D_VB = H_B * DV_B
D_MAIN = 3 * D_A + 2 * D_QKB + 2 * D_VB


def _dot(a, b):
    return jnp.dot(a, b, preferred_element_type=F32)


def _dot_nt(a, b):
    return lax.dot_general(a, b, (((1,), (1,)), ((), ())), preferred_element_type=F32)


def _dot_tn(a, b):
    return lax.dot_general(a, b, (((0,), (0,)), ((), ())), preferred_element_type=F32)


def _rms(x, g):
    return x * lax.rsqrt(jnp.mean(x * x, axis=-1, keepdims=True) + EPS) * g


def _params(*sem):
    return pltpu.CompilerParams(dimension_semantics=sem, vmem_limit_bytes=VMEM_LIMIT)


def _const_spec(shape):
    nd = len(shape)
    return pl.BlockSpec(shape, lambda *_: (0,) * nd, pipeline_mode=pl.Buffered(1))


def _inproj_kernel(x_ref, g_ref, w_ref, wlr_ref, wa2_ref, ba_ref,
                   qa_ref, ka_ref, va_ref, qb_ref, kb_ref, vb_ref, rb_ref, lf_ref, *kv_t_refs):
    tm = x_ref.shape[0]
    halves = (slice(0, tm // 2), slice(tm // 2, tm))
    xn_halves = [_rms(x_ref[rows, :], g_ref[...]).astype(BF16) for rows in halves]
    xn = jnp.concatenate(xn_halves, axis=0)
    gate_lane = lax.broadcasted_iota(jnp.int32, (tm // 2, LANES), 1) < GATE_RANK
    a_lr = jnp.concatenate([jnp.where(gate_lane, _dot(h, wlr_ref[...]), 0.0) for h in xn_halves], axis=0)
    def scaled(y, ref):
        if ref is qa_ref:
            return y * (HD_A ** -0.5 * LOG2_E)
        return y * (DK_B ** -0.5) if ref is qb_ref else y

    off = 0
    for ref in (qa_ref, ka_ref, va_ref, qb_ref, kb_ref, vb_ref, rb_ref):
        n = ref.shape[-1]
        if ref is qa_ref or ref is rb_ref:
            for rows, h in zip(halves, xn_halves):
                ref[rows, :] = scaled(_dot(h, w_ref[:, off:off + n]), ref)
        else:
            ref[...] = scaled(_dot(xn, w_ref[:, off:off + n]), ref)
        off += n
        if ref is qa_ref:
            z = _dot(a_lr.astype(BF16), wa2_ref[...]) + ba_ref[...]
            lf_ref[...] = (jnp.minimum(z, 0.0) - jnp.log(1.0 + jnp.exp(-jnp.abs(z)))) * (1.0 / GLA_TAU)
    if kv_t_refs:
        kt_ref, vt_ref = kv_t_refs
        kt_ref[...] = ka_ref[...].T
        vt_ref[...] = va_ref[...].T


def _inproj(x2, g_pre, w_in, w_a2, b_a, tm, seq=None):
    n, d = x2.shape
    widths = (D_A, D_A, D_A, D_QKB, D_QKB, D_VB, D_VB, D_QKB)
    assert w_in.shape[1] == D_MAIN + GATE_RANK and D_MAIN % LANES == 0
    w_main = pl.BlockSpec((d, D_MAIN), lambda i: (0, 0), pipeline_mode=pl.Buffered(1))
    w_lr = pl.BlockSpec((d, LANES), lambda i: (0, D_MAIN // LANES), pipeline_mode=pl.Buffered(1))
    row = lambda w: pl.BlockSpec((tm, w), lambda i: (i, 0))
    out_specs = [row(w) for w in widths]
    out_shape = [jax.ShapeDtypeStruct((n, w), F32) for w in widths]
    if seq is not None:
        per = seq // tm
        out_specs += [pl.BlockSpec((None, D_A, tm), lambda i: (i // per, 0, i % per))] * 2
        out_shape += [jax.ShapeDtypeStruct((n // seq, D_A, seq), F32)] * 2
    return pl.pallas_call(
        _inproj_kernel,
        grid=(n // tm,),
        in_specs=[row(d), _const_spec(g_pre.shape), w_main, w_lr, _const_spec(w_a2.shape), _const_spec(b_a.shape)],
        out_specs=out_specs,
        out_shape=out_shape,
        compiler_params=_params("parallel"),
        name="inproj",
    )(x2, g_pre, w_in, w_in, w_a2, b_a)


class _AttnUnit:
    def __init__(self, q_ref, k_ref, v_ref, dil, start, prev):
        n = BAND
        self.prev = prev

        def rows(s):
            if dil == 1:
                return pl.ds(pl.multiple_of(s, n), n)
            return pl.ds(s, n, stride=dil)

        self.rows = rows(start)
        is_a = lax.broadcasted_iota(jnp.int32, (n, LANES), 1) < HD_A
        q = q_ref[self.rows, :]
        q2 = jnp.concatenate([jnp.where(is_a, q, 0.0), jnp.where(is_a, 0.0, q)], axis=0).astype(BF16)
        self.key_rows = [self.rows]
        if prev is not None:
            self.key_rows.append(rows(start - n * dil if prev is True else jnp.where(prev, start - n * dil, start)))
        keys = jnp.concatenate([k_ref[r, :].astype(BF16) for r in self.key_rows], axis=0)
        self.scores = _dot_nt(q2, keys)
        self.v_ref = v_ref

    def softmax(self):
        n = BAND
        row = lax.broadcasted_iota(jnp.int32, (2 * n, n), 0) & (n - 1)
        col = lax.broadcasted_iota(jnp.int32, (2 * n, n), 1)
        s = self.scores
        s_own = jnp.where(col <= row, s[:, :n], NEG)
        m = s_own
        if self.prev is not None:
            reach = row if self.prev is True else row + jnp.where(self.prev, 0, n)
            s_prev = jnp.where(col >= reach, s[:, n:], NEG)
            m = jnp.maximum(s_own, s_prev)
        m = jnp.max(m, axis=-1, keepdims=True)
        p = [jnp.exp2(s_own - m).astype(BF16)]
        if self.prev is not None:
            p.append(jnp.exp2(s_prev - m).astype(BF16))
        self.probs = jnp.concatenate([pb[:n] for pb in p] + [pb[n:] for pb in p], axis=1)
        is_a = lax.broadcasted_iota(jnp.int32, (n, LANES), 1) < HD_A
        self.m = jnp.where(is_a, m[:n], m[n:])

    def weighted_values(self):
        is_a = lax.broadcasted_iota(jnp.int32, (BAND, LANES), 1) < HD_A
        ones_a = is_a.astype(BF16)
        zero = jnp.zeros((), BF16)
        v = [self.v_ref[r, :].astype(BF16) for r in self.key_rows]
        values = jnp.concatenate(
            [jnp.concatenate([jnp.where(is_a, vb, zero), ones_a], axis=1) for vb in v]
            + [jnp.concatenate([jnp.where(is_a, zero, vb), 1 - ones_a], axis=1) for vb in v], axis=0)
        out = _dot(self.probs, values)
        return self.rows, self.m, out[:, LANES:], out[:, :LANES]


def _attn_prompt_kernel(q_ref, k_ref, v_ref, o_ref, m_scr, l_scr, acc_scr, *, group, lag):
    seq = q_ref.shape[0]
    n = BAND
    order = sorted((dil for _, dil in PATTERNS), reverse=True)
    assert order[-1] == 1
    for p, dil in enumerate(order):
        nb = seq // (n * dil)
        assert (dil * nb) % group == 0 and (group % nb == 0 or nb % group == 0)

        def merge(sl, m2, l2, o2, p=p):
            if p == 0:
                m_scr[sl, :] = m2
                l_scr[sl, :] = l2
                acc_scr[sl, :] = o2
                return
            m0 = m_scr[sl, :]
            mn = jnp.maximum(m0, m2)
            a0 = jnp.exp2(m0 - mn)
            a2 = jnp.exp2(m2 - mn)
            l = a0 * l_scr[sl, :] + a2 * l2
            acc = a0 * acc_scr[sl, :] + a2 * o2
            if p == len(order) - 1:
                o_ref[sl, :] = (acc / l).astype(o_ref.dtype)
            else:
                m_scr[sl, :] = mn
                l_scr[sl, :] = l
                acc_scr[sl, :] = acc

        def units(i, carry, dil=dil, nb=nb, merge=merge):
            where = []
            for j in range(group):
                if group % nb == 0:
                    r = i * (group // nb) + j // nb
                    c = j % nb
                    prev = None if nb == 1 else c > 0
                    if prev is False:
                        prev = None
                else:
                    r = (i * group) // nb
                    c = (i * group) % nb + j
                    prev = True if j > 0 else c > 0
                where.append((r + c * (n * dil), prev))
            inflight = {}
            for s in range(group + 2 * lag):
                if s < group:
                    inflight[s] = _AttnUnit(q_ref, k_ref, v_ref, dil, *where[s])
                if 0 <= s - lag < group:
                    inflight[s - lag].softmax()
                if 0 <= s - 2 * lag < group:
                    merge(*inflight.pop(s - 2 * lag).weighted_values())
            return carry

        lax.fori_loop(0, dil * nb // group, units, 0)


def _attn_prompt(qa, ka, va, group=16, lag=1):
    b, s, _ = qa.shape
    assert s % (BAND * max(d for _, d in PATTERNS)) == 0
    blk = pl.BlockSpec((None, s, LANES), lambda i, j: (i, 0, j))
    return pl.pallas_call(
        functools.partial(_attn_prompt_kernel, group=group, lag=lag),
        grid=(b, D_A // LANES),
        in_specs=[blk, blk, blk],
        out_specs=blk,
        out_shape=jax.ShapeDtypeStruct((b, s, D_A), BF16),
        scratch_shapes=[pltpu.VMEM((s, LANES), F32)] * 3,
        compiler_params=_params("parallel", "parallel"),
        name="attn_prompt",
    )(qa, ka, va)


def _pattern_count(dist):
    cnt = jnp.zeros(dist.shape, jnp.int32)
    for _, dil in PATTERNS:
        shift = dil.bit_length() - 1
        hit = jnp.logical_and((dist & (dil - 1)) == 0, (dist >> shift) <= BAND)
        cnt = cnt + hit.astype(jnp.int32)
    return jnp.where(dist >= 0, cnt, 0)


def _attn_sample_kernel(q_ref, kn_ref, vn_ref, ck_ref, cv_ref, o_ref):
    t = q_ref.shape[0]
    lb = ck_ref.shape[1]
    is_a = lax.broadcasted_iota(jnp.int32, (t, LANES), 1) < HD_A
    tq_c = lax.broadcasted_iota(jnp.int32, (2 * t, lb), 0) % t
    kp_c = lax.broadcasted_iota(jnp.int32, (2 * t, lb), 1)
    cnt_c = _pattern_count(lb + tq_c - kp_c)
    tq_n = lax.broadcasted_iota(jnp.int32, (2 * t, t), 0) % t
    kp_n = lax.broadcasted_iota(jnp.int32, (2 * t, t), 1)
    cnt_n = _pattern_count(tq_n - kp_n)
    w_c, w_n = cnt_c.astype(F32), cnt_n.astype(F32)

    pairs = [slice(j * LANES, (j + 1) * LANES) for j in range(D_A // LANES)]
    scores = []
    for cols in pairs:
        q = q_ref[:, cols]
        q2 =jnp.concatenate([jnp.where(is_a, q, 0.0), jnp.where(is_a, 0.0, q)], axis=0).astype(BF16)
        scores.append((_dot(q2, ck_ref[cols, :].astype(BF16)),
                       _dot_nt(q2, kn_ref[:, cols].astype(BF16))))
    probs = []
    for s_c, s_n in scores:
        m = jnp.maximum(jnp.max(jnp.where(cnt_c > 0, s_c, NEG), axis=-1, keepdims=True),
                        jnp.max(jnp.where(cnt_n > 0, s_n, NEG), axis=-1, keepdims=True))
        e_c = jnp.where(cnt_c > 0, jnp.exp2(s_c - m), 0.0) * w_c
        e_n = jnp.where(cnt_n > 0, jnp.exp2(s_n - m), 0.0) * w_n
        l = jnp.sum(e_c, axis=-1, keepdims=True) + jnp.sum(e_n, axis=-1, keepdims=True)
        probs.append((e_c.astype(BF16), e_n.astype(BF16), l))
    for cols, (e_c, e_n, l) in zip(pairs, probs):
        o = (_dot_nt(e_c, cv_ref[cols, :].astype(BF16)) + _dot(e_n, vn_ref[:, cols].astype(BF16))) / l
        o_ref[:, cols] = jnp.where(is_a, o[:t], o[t:]).astype(o_ref.dtype)


def _attn_sample(qa, ka, va, ck, cv):
    b, t, _ = qa.shape
    lb = ck.shape[2]
    new = pl.BlockSpec((None, t, D_A), lambda i: (i, 0, 0))
    buf = pl.BlockSpec((None, D_A, lb), lambda i: (i, 0, 0))
    return pl.pallas_call(
        _attn_sample_kernel,
        grid=(b,),
        in_specs=[new, new, new, buf, buf],
        out_specs=new,
        out_shape=jax.ShapeDtypeStruct((b, t, D_A), BF16),
        compiler_params=_params("parallel"),
        name="attn_sample",
    )(qa, ka, va, ck, cv)


def _gla_kernel(q_ref, k_ref, lf_ref, v_ref, r_ref, gn_ref, st0_ref, o_ref, st_ref,
                qe_scr, ke_scr, qs_scr, kd_scr, dec_scr, *, group, scan):
    t = q_ref.shape[0]
    c = GLA_CHUNK
    nc = t // c
    half = c // 2

    b = lf_ref[...]
    in_chunk = lax.broadcasted_iota(jnp.int32, (t, LANES), 0) & (c - 1)
    step = 1
    while step < c:
        b = b + jnp.where(in_chunk >= step, pltpu.roll(b, step, axis=0), 0.0)
        step *= 2
    b3 = b.reshape(nc, c, LANES)
    mid = b3[:, half - 1:half, :]
    last = b3[:, c - 1:c, :]
    q3 = q_ref[...].reshape(nc, c, LANES)
    k3 = k_ref[...].reshape(nc, c, LANES)
    qe_scr[...] = (q3 * jnp.exp(b3 - mid)).reshape(t, LANES).astype(BF16)
    ke_scr[...] = (k3 * jnp.exp(mid - b3)).reshape(t, LANES).astype(BF16)
    qs_scr[...] = (q3 * jnp.exp(b3)).reshape(t, LANES).astype(BF16)
    kd_scr[...] = (k3 * jnp.exp(last - b3)).reshape(t, LANES).astype(BF16)
    dec_scr[...] = jnp.broadcast_to(jnp.exp(last), (nc, 8, LANES))

    causal2 = (lax.broadcasted_iota(jnp.int32, (2 * c, c), 1)
               <= (lax.broadcasted_iota(jnp.int32, (2 * c, c), 0) & (c - 1)))
    is_a = lax.broadcasted_iota(jnp.int32, (c, LANES), 1) < DK_B
    gn = gn_ref[...]

    def both_heads(x):
        return jnp.concatenate([jnp.where(is_a, x, 0), jnp.where(is_a, 0, x)], axis=0)

    def chunks(i, st):
        base = i * group
        rows = [pl.ds(pl.multiple_of((base + j) * c, c), c) for j in range(group)]
        vals, att, upd = [], [], []
        for rs in rows:
            v = v_ref[rs, :]
            vals.append((v[:, :DV_B].astype(BF16), v[:, DV_B:].astype(BF16)))
            att.append(_dot_nt(both_heads(qe_scr[rs, :]), ke_scr[rs, :]))
            upd.append(_dot_tn(jnp.concatenate(vals[-1], axis=0), both_heads(kd_scr[rs, :])))
        intra = []
        for (va, vb), a in zip(vals, att):
            a = jnp.where(causal2, a, 0.0).astype(BF16)
            intra.append(jnp.concatenate([_dot(a[:c], va), _dot(a[c:], vb)], axis=1))
        before = []
        for j in range(group):
            if not scan:
                st = st0_ref[base + j]
            before.append(st.astype(BF16))
            st = dec_scr[base + j][0:1, :] * st + upd[j]
            if not scan:
                st_ref[base + j] = st
        for j, rs in enumerate(rows):
            inter = _dot_nt(both_heads(qs_scr[rs, :]), before[j])
            o = intra[j] + jnp.concatenate([inter[:c], inter[c:]], axis=1)
            r = r_ref[rs, :]
            for e in range(2):
                cols = slice(e * DV_B, (e + 1) * DV_B)
                y = _rms(o[:, cols], gn[:, cols]) * (r[:, cols] * jax.nn.sigmoid(r[:, cols]))
                o_ref[rs, cols] = y.astype(o_ref.dtype)
        return st

    if scan:
        st_ref[...] = lax.fori_loop(0, nc // group, chunks, st0_ref[...])
    else:
        lax.fori_loop(0, nc // group, chunks, jnp.zeros((DV_B, LANES), F32))


def _gla(qb, kb, lf, vb, rb, g_norm, st0, group, scan):
    b, t, _ = qb.shape
    assert t % (GLA_CHUNK * group) == 0
    qk = pl.BlockSpec((None, t, LANES), lambda i, j: (i, 0, j))
    vr = pl.BlockSpec((None, t, 2 * DV_B), lambda i, j: (i, 0, j))
    if scan:
        st = pl.BlockSpec((None, None, DV_B, LANES), lambda i, j: (i, j, 0, 0))
    else:
        assert b == 1 and st0.shape[0] == t // GLA_CHUNK
        st = pl.BlockSpec((t // GLA_CHUNK, None, DV_B, LANES), lambda i, j: (0, j, 0, 0))
    return pl.pallas_call(
        functools.partial(_gla_kernel, group=group, scan=scan),
        grid=(b, H_B // 2),
        in_specs=[qk, qk, qk, vr, vr, pl.BlockSpec((1, 2 * DV_B), lambda i, j: (0, j)), st],
        out_specs=[vr, st],
        out_shape=[jax.ShapeDtypeStruct((b, t, D_VB), BF16), jax.ShapeDtypeStruct(st0.shape, F32)],
        scratch_shapes=[pltpu.VMEM((t, LANES), BF16)] * 4 + [pltpu.VMEM((t // GLA_CHUNK, 8, LANES), F32)],
        compiler_params=_params("parallel", "parallel"),
        name="gla",
    )(qb, kb, lf, vb, rb, g_norm, st0)


def _state_to_pairs(s):
    b = s.shape[0]
    return s.reshape(b, H_B // 2, 2, DK_B, DV_B).transpose(0, 1, 4, 2, 3).reshape(b, H_B // 2, DV_B, 2 * DK_B)


def _pairs_to_state(sp):
    b = sp.shape[0]
    return sp.reshape(b, H_B // 2, DV_B, 2, DK_B).transpose(0, 1, 3, 4, 2).reshape(b, H_B, DK_B, DV_B)


def _gelu(x):
    return 0.5 * x * (1.0 + lax.erf(x * (2.0 ** -0.5)))


def _mix_ffn(oa_ref, ob_ref, x_ref, wo_ref, gains_ref, wg_ref, wu_ref, wd_ref, cw_ref, cb_ref, act_ref, out_ref,
             fc, parts, shifted, emit_g):
    heads = []
    for rows in parts:
        mixed = _dot(oa_ref[rows, :], wo_ref[0:D_A, :]) + _dot(ob_ref[rows, :], wo_ref[D_A:, :])
        x1 = x_ref[rows, :] + _rms(mixed, gains_ref[0:1, :])
        heads.append((x1, _rms(x1, gains_ref[1:2, :]).astype(BF16)))
    d_ff = wg_ref.shape[1]
    for rows, (x1, h) in zip(parts, heads):
        for ci in range(d_ff // fc):
            sl = slice(ci * fc, (ci + 1) * fc)
            g = _dot(h, wg_ref[:, sl])
            u = _dot(h, wu_ref[:, sl])
            g1, g2 = shifted(g, sl, rows)
            conv = cb_ref[:, sl] + cw_ref[0:1, sl] * g2 + cw_ref[1:2, sl] * g1 + cw_ref[2:3, sl] * g
            act_ref[rows, sl] = (_gelu(conv) * u).astype(BF16)
            emit_g(g, sl, rows)
        out_ref[rows, :] = x1 + _rms(_dot(act_ref[rows, :], wd_ref[...]), gains_ref[2:3, :])


def _ffn_prompt_kernel(oa_ref, ob_ref, x_ref, wo_ref, gains_ref, wg_ref, wu_ref, wd_ref, cw_ref, cb_ref,
                       out_ref, cst_ref, carry_ref, act_ref, *, fc, n_parts):
    tm = x_ref.shape[0]
    part = tm // n_parts

    @pl.when(pl.program_id(1) == 0)
    def _():
        carry_ref[...] = jnp.zeros_like(carry_ref)

    row = lax.broadcasted_iota(jnp.int32, (part, fc), 0)

    def shifted(g, sl, rows):
        prev = carry_ref[:, sl]
        g1 = jnp.where(row == 0, prev[1:2], pltpu.roll(g, 1, axis=0))
        g2 = jnp.where(row == 0, prev[0:1], jnp.where(row == 1, prev[1:2], pltpu.roll(g, 2, axis=0)))
        return g1, g2

    def emit_g(g, sl, rows):
        tail = g[part - 2:part]
        carry_ref[0:2, sl] = tail
        cst_ref[:, sl] = tail

    _mix_ffn(oa_ref, ob_ref, x_ref, wo_ref, gains_ref, wg_ref, wu_ref, wd_ref, cw_ref, cb_ref, act_ref, out_ref,
             fc, [slice(i * part, (i + 1) * part) for i in range(n_parts)], shifted, emit_g)


def _up_halves(w_up):
    d, two_ff = w_up.shape
    return [pl.BlockSpec((d, two_ff // 2), lambda *_, j=j: (0, j), pipeline_mode=pl.Buffered(1)) for j in range(2)]


def _ffn_prompt(oa, ob, x, w_o, gains, w_up, wd, conv_w, conv_b, tm, fc, n_parts):
    b, s, d = x.shape
    d_ff = wd.shape[0]
    tok = lambda w: pl.BlockSpec((None, tm, w), lambda i, j: (i, j, 0))
    return pl.pallas_call(
        functools.partial(_ffn_prompt_kernel, fc=fc, n_parts=n_parts),
        grid=(b, s // tm),
        in_specs=[tok(D_A), tok(D_VB), tok(d), _const_spec(w_o.shape), _const_spec(gains.shape)]
        + _up_halves(w_up) + [_const_spec(a.shape) for a in (wd, conv_w, conv_b)],
        out_specs=[tok(d), pl.BlockSpec((None, 2, d_ff), lambda i, j: (i, 0, 0))],
        out_shape=[jax.ShapeDtypeStruct((b, s, d), F32), jax.ShapeDtypeStruct((b, 2, d_ff), F32)],
        scratch_shapes=[pltpu.VMEM((8, d_ff), F32), pltpu.VMEM((tm, d_ff), BF16)],
        compiler_params=_params("parallel", "arbitrary"),
        name="ffn_prompt",
    )(oa, ob, x, w_o, gains, w_up, w_up, wd, conv_w, conv_b)


def _ffn_sample_kernel(oa_ref, ob_ref, x_ref, p1_ref, p2_ref, wo_ref, gains_ref, wg_ref, wu_ref, wd_ref,
                       cw_ref, cb_ref, out_ref, g_ref, act_ref, *, fc, seg):
    tm = x_ref.shape[0]
    pos = lax.broadcasted_iota(jnp.int32, (tm, fc), 0) % seg

    def shifted(g, sl, rows):
        g1 = jnp.where(pos >= 1, pltpu.roll(g, 1, axis=0), p1_ref[rows, sl])
        g2 = jnp.where(pos >= 2, pltpu.roll(g, 2, axis=0), p2_ref[rows, sl])
        return g1, g2

    def emit_g(g, sl, rows):
        g_ref[rows, sl] = g

    _mix_ffn(oa_ref, ob_ref, x_ref, wo_ref, gains_ref, wg_ref, wu_ref, wd_ref, cw_ref, cb_ref, act_ref, out_ref,
             fc, [slice(0, tm)], shifted, emit_g)


def _ffn_sample(oa, ob, x, conv_state, w_o, gains, w_up, wd, conv_w, conv_b, fc):
    b, t, d = x.shape
    d_ff = wd.shape[0]
    assert t >= 2
    zeros = jnp.zeros((b, t - 2, d_ff), F32)
    p1 = jnp.concatenate([conv_state[:, 1:2], zeros, zeros[:, :1]], axis=1).reshape(b * t, d_ff)
    p2 = jnp.concatenate([conv_state, zeros], axis=1).reshape(b * t, d_ff)
    head = (oa.reshape(b * t, -1), ob.reshape(b * t, -1), x.reshape(b * t, d), p1, p2, w_o, gains)
    tail = (wd, conv_w, conv_b)
    args = head + (w_up, w_up) + tail
    x2, g = pl.pallas_call(
        functools.partial(_ffn_sample_kernel, fc=fc, seg=t),
        grid=(1,),
        in_specs=[_const_spec(a.shape) for a in head] + _up_halves(w_up) + [_const_spec(a.shape) for a in tail],
        out_specs=[_const_spec((b * t, d)), _const_spec((b * t, d_ff))],
        out_shape=[jax.ShapeDtypeStruct((b * t, d), F32), jax.ShapeDtypeStruct((b * t, d_ff), F32)],
        scratch_shapes=[pltpu.VMEM((b * t, d_ff), BF16)],
        compiler_params=_params("arbitrary"),
        name="ffn_sample",
    )(*args)
    return x2.reshape(b, t, d), g.reshape(b, t, d_ff)


def _pick_tile(n, want):
    tm = min(n, want)
    assert n % tm == 0
    return tm


def _layer(x, weights, tm_want, feature_major_kv):
    b, t, d = x.shape
    x2 = x.reshape(b * t, d)
    tm = _pick_tile(t if feature_major_kv else b * t, tm_want)
    outs = _inproj(x2, weights["g_pre_mix"], weights["w_in"], weights["w_a2"], weights["b_a"], tm,
                   seq=t if feature_major_kv else None)
    proj = [o.reshape(b, t, -1) for o in outs[:8]]
    if feature_major_kv:
        win = [o.reshape(b, H_A, HD_A, t).transpose(0, 3, 1, 2)[None] for o in outs[8:]]
    else:
        win = [proj[1].reshape(1, b, t, H_A, HD_A), proj[2].reshape(1, b, t, H_A, HD_A)]
    return proj + win


def kernel(x_prompt, x_sample, cache_k_win, cache_v_win, state_gla, state_ffn_conv,
           w_in, w_a2, b_a, g_gla_norm, w_o, g_pre_mix, g_post_mix, g_pre_ffn, g_post_ffn,
           w_up, conv_w, conv_b, w_down):
    depth = w_in.shape[0]
    assert depth == 1
    l = 0
    d = x_prompt.shape[-1]
    d_ff = w_down.shape[1]
    fc = 256
    assert d_ff % fc == 0

    w_in_b = w_in[l].astype(BF16)
    wts = {
        "w_in": w_in_b,
        "w_a2": jnp.pad(w_a2[l].astype(BF16), ((0, LANES - GATE_RANK), (0, 0))),
        "b_a": b_a[l].reshape(1, -1),
        "g_pre_mix": g_pre_mix[l].reshape(1, -1),
    }
    g_norm = g_gla_norm[l].reshape(1, -1)
    w_o_b = w_o[l].astype(BF16)
    gains = jnp.stack([g_post_mix[l], g_pre_ffn[l], g_post_ffn[l]])
    w_up_b = w_up[l].astype(BF16)
    wd = w_down[l].astype(BF16)
    cw = conv_w[l]
    cb = conv_b[l].reshape(1, -1)

    bp, sp, _ = x_prompt.shape
    qa, ka, va, qb, kb, vb, rb, lf, k_win, v_win = _layer(x_prompt, wts, 512, feature_major_kv=True)
    oa = _attn_prompt(qa, ka, va)
    st0 = jnp.zeros((bp, H_B // 2, DV_B, 2 * DK_B), F32)
    ob, st = _gla(qb, kb, lf, vb, rb, g_norm, st0, group=sp // GLA_CHUNK, scan=True)
    y_prompt, conv_p = _ffn_prompt(oa, ob, x_prompt, w_o_b, gains, w_up_b, wd, cw, cb, _pick_tile(sp, 1024), fc,
                                   n_parts=2)
    buf = min(max(w for w, _ in PATTERNS), sp)
    win_k_p = k_win[:, :, sp - buf:]
    win_v_p = v_win[:, :, sp - buf:]
    gla_p = _pairs_to_state(st)[None]
    conv_p = conv_p[None]

    bs, ts, _ = x_sample.shape
    qa, ka, va, qb, kb, vb, rb, lf, win_k_s, win_v_s = _layer(x_sample, wts, 256, feature_major_kv=False)
    lbuf = cache_k_win.shape[2]
    feature_major = lambda c: c.transpose(0, 2, 3, 1).reshape(bs, D_A, lbuf)
    oa = _attn_sample(qa, ka, va, feature_major(cache_k_win[l]), feature_major(cache_v_win[l]))
    assert ts <= GLA_CHUNK
    padt = lambda a: jnp.pad(a, ((0, 0), (0, GLA_CHUNK - ts), (0, 0))).reshape(1, bs * GLA_CHUNK, -1)
    ob, st = _gla(padt(qb), padt(kb), padt(lf), padt(vb), padt(rb), g_norm, _state_to_pairs(state_gla[l]),
                  group=math.gcd(bs, 16), scan=False)
    ob = ob.reshape(bs, GLA_CHUNK, -1)[:, :ts]
    y_sample, g_s = _ffn_sample(oa, ob, x_sample, state_ffn_conv[l], w_o_b, gains, w_up_b, wd, cw, cb, fc)
    gla_s = _pairs_to_state(st)[None]
    conv_s = g_s[:, ts - 2:][None]

    return (y_prompt, y_sample, win_k_p, win_v_p, gla_p, conv_p, win_k_s, win_v_s, gla_s, conv_s)
```
